```python
import math
import jax, jax.numpy as jnp
from jax import lax
import numpy as np

D_MODEL = 4096
BATCH = 2
SEQ = 4096
DEPTH = 1

MEM_LEN = 256
SWA_HEADS = 16
SWA_KV_HEADS = 4
SWA_HEAD_DIM = 128
WINDOW = 128
BLOCK = 128
M_HEADS = 4
M_QK_DIM = 128
M_V_DIM = 256
M_CHUNK = 128
CONV_WIDTH = 4
X_HEADS = 4
X_HEAD_DIM = 256
NUM_BUCKETS = 32
MAX_DISTANCE = 128

SWA_W = SWA_HEADS * SWA_HEAD_DIM
SWA_KV_W = SWA_KV_HEADS * SWA_HEAD_DIM
M_QK_W = M_HEADS * M_QK_DIM
M_V_W = M_HEADS * M_V_DIM
X_W = X_HEADS * X_HEAD_DIM
MIX_W = SWA_W + M_V_W + X_W

IN_SPLITS = (
    SWA_W, SWA_KV_W, SWA_KV_W, SWA_W,
    M_QK_W, M_QK_W, M_V_W, M_HEADS, M_HEADS, M_V_W, M_V_W,
    X_W, X_W,
)
IN_W = sum(IN_SPLITS)

DEEPNORM_ALPHA = (2.0 * DEPTH) ** 0.25
DEEPNORM_BETA = (8.0 * DEPTH) ** -0.25
LN_EPS = 1e-5
HEAD_NORM_EPS = 1e-6

kernel_name = "hymba_swa_mlstm_memxattn_deepnorm"


def _split_points(sizes):
    pts, acc = [], 0
    for s in sizes[:-1]:
        acc += s
        pts.append(acc)
    return pts


def layer_norm(x, g, b):
    xf = x.astype(jnp.float32)
    mu = jnp.mean(xf, axis=-1, keepdims=True)
    var = jnp.mean(jnp.square(xf - mu), axis=-1, keepdims=True)
    return (xf - mu) * lax.rsqrt(var + LN_EPS) * g.astype(jnp.float32) + b.astype(jnp.float32)


def t5_bucket(dist):
    max_exact = NUM_BUCKETS // 2
    is_small = dist < max_exact
    ratio = jnp.maximum(dist, max_exact).astype(jnp.float32) / max_exact
    large = max_exact + (jnp.log(ratio) / math.log(MAX_DISTANCE / max_exact)
                         * (NUM_BUCKETS - max_exact)).astype(jnp.int32)
    large = jnp.minimum(large, NUM_BUCKETS - 1)
    return jnp.where(is_small, dist, large)


def causal_depthwise_conv(x, w, b):
    c = x.shape[-1]
    y = lax.conv_general_dilated(
        x, w[:, None, :].astype(x.dtype), window_strides=(1,),
        padding=((CONV_WIDTH - 1, 0),),
        dimension_numbers=("NWC", "WIO", "NWC"),
        feature_group_count=c)
    return y + b.astype(x.dtype)


def swa_attention(q, k, v, rel_bias, sinks):
    B, S = q.shape[0], q.shape[1]
    nb = S // BLOCK
    G = SWA_HEADS // SWA_KV_HEADS
    q = q.reshape(B, nb, BLOCK, SWA_KV_HEADS, G, SWA_HEAD_DIM)
    k = k.reshape(B, nb, BLOCK, SWA_KV_HEADS, SWA_HEAD_DIM)
    v = v.reshape(B, nb, BLOCK, SWA_KV_HEADS, SWA_HEAD_DIM)
    shift = lambda t: jnp.concatenate([jnp.zeros_like(t[:, :1]), t[:, :-1]], axis=1)
    kk = jnp.concatenate([shift(k), k], axis=2)
    vv = jnp.concatenate([shift(v), v], axis=2)
    s = jnp.einsum("bnqkgd,bnskd->bnkgqs", q, kk).astype(jnp.float32) * (SWA_HEAD_DIM ** -0.5)

    qi = jnp.arange(BLOCK, dtype=jnp.int32)[:, None]
    si = jnp.arange(2 * BLOCK, dtype=jnp.int32)[None, :]
    dist = qi + BLOCK - si
    in_win = (dist >= 0) & (dist < WINDOW)
    first = (jnp.arange(nb) == 0)[:, None, None]
    valid = in_win[None] & ~(first & (si < BLOCK)[None])

    bucket = t5_bucket(jnp.clip(dist, 0, MAX_DISTANCE - 1))
    bias = rel_bias[bucket].astype(jnp.float32)
    bias = bias.transpose(2, 0, 1).reshape(SWA_KV_HEADS, G, BLOCK, 2 * BLOCK)
    s = jnp.where(valid[None, :, None, None], s + bias, -jnp.inf)

    sink = sinks.astype(jnp.float32).reshape(SWA_KV_HEADS, G)[None, None, :, :, None, None]
    m = jnp.maximum(jnp.max(s, axis=-1, keepdims=True), sink)
    p = jnp.exp(s - m)
    denom = jnp.sum(p, axis=-1, keepdims=True) + jnp.exp(sink - m)
    o = jnp.einsum("bnkgqs,bnskd->bnqkgd", (p / denom).astype(vv.dtype), vv)
    return o.reshape(B, S, SWA_W)


def mlstm_chunkwise(q, k, v, i_pre, f_pre):
    B, S = q.shape[0], q.shape[1]
    nc, L = S // M_CHUNK, M_CHUNK
    to_chunks = lambda t: jnp.moveaxis(
        t.astype(jnp.float32).reshape((B, nc, L) + t.shape[2:]), 3, 1)
    q = to_chunks(q)
    k = to_chunks(k) * (M_QK_DIM ** -0.5)
    v = to_chunks(v)
    log_i = to_chunks(i_pre)
    log_f = jax.nn.log_sigmoid(to_chunks(f_pre))

    b = jnp.cumsum(log_f, axis=-1)
    g = b[..., -1]
    a = g[..., None] - b + log_i
    m_loc = jnp.max(a, axis=-1)
    w = jnp.exp(a - m_loc[..., None])
    C_loc = jnp.einsum("bhclv,bhcld->bhcvd", w[..., None] * v, k)
    n_loc = jnp.einsum("bhcl,bhcld->bhcd", w, k)

    def step(carry, xs):
        C, n, m = carry
        g_c, m_l, C_l, n_l = xs
        m_new = jnp.maximum(g_c + m, m_l)
        sp = jnp.exp(g_c + m - m_new)
        sl = jnp.exp(m_l - m_new)
        C_new = sp[..., None, None] * C + sl[..., None, None] * C_l
        n_new = sp[..., None] * n + sl[..., None] * n_l
        return (C_new, n_new, m_new), (C, n, m)

    init = (jnp.zeros((B, M_HEADS, M_V_DIM, M_QK_DIM), jnp.float32),
            jnp.zeros((B, M_HEADS, M_QK_DIM), jnp.float32),
            jnp.zeros((B, M_HEADS), jnp.float32))
    xs = (jnp.moveaxis(g, 2, 0), jnp.moveaxis(m_loc, 2, 0),
          jnp.moveaxis(C_loc, 2, 0), jnp.moveaxis(n_loc, 2, 0))
    _, (C_prev, n_prev, m_prev) = lax.scan(step, init, xs)
    C_prev = jnp.moveaxis(C_prev, 0, 2)
    n_prev = jnp.moveaxis(n_prev, 0, 2)
    m_prev = jnp.moveaxis(m_prev, 0, 2)

    causal = jnp.tril(jnp.ones((L, L), dtype=bool))
    D = b[..., :, None] - b[..., None, :] + log_i[..., None, :]
    D = jnp.where(causal, D, -jnp.inf)
    inter = b + m_prev[..., None]
    m_t = jnp.maximum(inter, jnp.max(D, axis=-1))
    Sm = jnp.einsum("bhctd,bhcsd->bhcts", q, k) * jnp.exp(D - m_t[..., None])
    w_inter = jnp.exp(inter - m_t)
    num = (jnp.einsum("bhcts,bhcsv->bhctv", Sm, v)
           + w_inter[..., None] * jnp.einsum("bhctd,bhcvd->bhctv", q, C_prev))
    den = jnp.sum(Sm, axis=-1) + w_inter * jnp.einsum("bhctd,bhcd->bhct", q, n_prev)
    h = num / jnp.maximum(jnp.abs(den), jnp.exp(-m_t))[..., None]
    return jnp.moveaxis(h, 1, 3).reshape(B, S, M_HEADS, M_V_DIM)


def setup_inputs(seed: int = 0) -> dict:
    key = jax.random.key(seed)
    ks = jax.random.split(key, 14)
    f32 = jnp.float32
    x = jax.random.normal(ks[0], (BATCH, SEQ, D_MODEL), f32)
    mem = jax.random.normal(ks[1], (BATCH, MEM_LEN, D_MODEL), f32)
    w_in = jax.random.normal(ks[2], (D_MODEL, IN_W), f32) * D_MODEL ** -0.5
    conv_w = jax.random.normal(ks[3], (CONV_WIDTH, 2 * M_QK_W), f32) * CONV_WIDTH ** -0.5
    conv_b = 0.01 * jax.random.normal(ks[4], (2 * M_QK_W,), f32)
    b_i = 0.1 * jax.random.normal(ks[5], (M_HEADS,), f32)
    b_f = 3.0 + 3.0 * jax.random.uniform(ks[6], (M_HEADS,), f32)
    m_norm_g = 1.0 + 0.02 * jax.random.normal(ks[7], (M_V_W,), f32)
    rel_bias = 0.1 * jax.random.normal(ks[8], (NUM_BUCKETS, SWA_HEADS), f32)
    sinks = jax.random.normal(ks[9], (SWA_HEADS,), f32)
    w_mem_kv = jax.random.normal(ks[10], (D_MODEL, 2 * X_W), f32) * D_MODEL ** -0.5
    w_out = jax.random.normal(ks[11], (MIX_W, D_MODEL), f32) * (MIX_W ** -0.5) * DEEPNORM_BETA
    ln_g = 1.0 + 0.02 * jax.random.normal(ks[12], (D_MODEL,), f32)
    ln_b = 0.02 * jax.random.normal(ks[13], (D_MODEL,), f32)
    return {"x": x, "mem": mem, "w_in": w_in, "conv_w": conv_w, "conv_b": conv_b,
            "b_i": b_i, "b_f": b_f, "m_norm_g": m_norm_g, "rel_bias": rel_bias,
            "sinks": sinks, "w_mem_kv": w_mem_kv, "w_out": w_out,
            "ln_g": ln_g, "ln_b": ln_b}


def reference(x, mem, w_in, conv_w, conv_b, b_i, b_f, m_norm_g, rel_bias, sinks,
              w_mem_kv, w_out, ln_g, ln_b):
    B, S, _ = x.shape
    for _layer in range(DEPTH):
        proj = jnp.einsum("bsd,de->bse", x, w_in)
        (a_q, a_k, a_v, a_z, m_q, m_k, m_v, m_i, m_f, m_o, m_z, c_q, c_z) = jnp.split(
            proj, _split_points(IN_SPLITS), axis=-1)

        a_out = swa_attention(
            a_q.reshape(B, S, SWA_HEADS, SWA_HEAD_DIM),
            a_k.reshape(B, S, SWA_KV_HEADS, SWA_HEAD_DIM),
            a_v.reshape(B, S, SWA_KV_HEADS, SWA_HEAD_DIM),
            rel_bias, sinks)
        a_out = a_out.astype(jnp.float32) * jax.nn.silu(a_z.astype(jnp.float32))

        qk = jax.nn.silu(causal_depthwise_conv(jnp.concatenate([m_q, m_k], axis=-1), conv_w, conv_b))
        mq, mk = jnp.split(qk, [M_QK_W], axis=-1)
        i_pre = m_i.astype(jnp.float32) + b_i.astype(jnp.float32)
        f_pre = m_f.astype(jnp.float32) + b_f.astype(jnp.float32)
        h = mlstm_chunkwise(
            mq.reshape(B, S, M_HEADS, M_QK_DIM), mk.reshape(B, S, M_HEADS, M_QK_DIM),
            m_v.reshape(B, S, M_HEADS, M_V_DIM), i_pre, f_pre)
        h = jax.nn.sigmoid(m_o.astype(jnp.float32)).reshape(B, S, M_HEADS, M_V_DIM) * h
        mu = jnp.mean(h, axis=-1, keepdims=True)
        var = jnp.mean(jnp.square(h - mu), axis=-1, keepdims=True)
        h = (h - mu) * lax.rsqrt(var + HEAD_NORM_EPS) * m_norm_g.astype(jnp.float32).reshape(M_HEADS, M_V_DIM)
        m_out = h.reshape(B, S, M_V_W) * jax.nn.silu(m_z.astype(jnp.float32))

        mkv = jnp.einsum("bmd,de->bme", mem, w_mem_kv)
        mem_k, mem_v = jnp.split(mkv, [X_W], axis=-1)
        mem_k = mem_k.reshape(B, MEM_LEN, X_HEADS, X_HEAD_DIM)
        mem_v = mem_v.reshape(B, MEM_LEN, X_HEADS, X_HEAD_DIM)
        cs = jnp.einsum("bqhd,bmhd->bhqm", c_q.reshape(B, S, X_HEADS, X_HEAD_DIM),
                        mem_k).astype(jnp.float32) * (X_HEAD_DIM ** -0.5)
        cp = jax.nn.softmax(cs, axis=-1)
        c_out = jnp.einsum("bhqm,bmhd->bqhd", cp.astype(mem_v.dtype), mem_v).reshape(B, S, X_W)
        c_out = c_out.astype(jnp.float32) * jax.nn.silu(c_z.astype(jnp.float32))

        mix = jnp.concatenate([a_out, m_out, c_out], axis=-1).astype(x.dtype)
        y = jnp.einsum("bse,ed->bsd", mix, w_out)
        x = layer_norm(DEEPNORM_ALPHA * x.astype(jnp.float32) + y.astype(jnp.float32),
                       ln_g, ln_b).astype(x.dtype)
    return x
```

```python
import functools
import math

import jax
import jax.numpy as jnp
import numpy as np
from jax import lax
from jax.experimental import pallas as pl
from jax.experimental.pallas import tpu as pltpu

F32 = jnp.float32
BF16 = jnp.bfloat16

D_MODEL = 4096
MEM_LEN = 256
SWA_HEADS = 16
SWA_KV_HEADS = 4
SWA_GROUP = SWA_HEADS // SWA_KV_HEADS
SWA_HEAD_DIM = 128
WINDOW = 128
BLOCK = 128
M_HEADS = 4
M_QK_DIM = 128
M_V_DIM = 256
M_CHUNK = 128
CONV_WIDTH = 4
X_HEADS = 4
X_HEAD_DIM = 256
NUM_BUCKETS = 32
MAX_DISTANCE = 128

SWA_W = SWA_HEADS * SWA_HEAD_DIM
SWA_KV_W = SWA_KV_HEADS * SWA_HEAD_DIM
M_QK_W = M_HEADS * M_QK_DIM
M_V_W = M_HEADS * M_V_DIM
X_W = X_HEADS * X_HEAD_DIM
MIX_W = SWA_W + M_V_W + X_W

DEEPNORM_ALPHA = 2.0 ** 0.25
LN_EPS = 1e-5
HEAD_NORM_EPS = 1e-6

_REF_SPLITS = (
    ("a_q", SWA_W), ("a_k", SWA_KV_W), ("a_v", SWA_KV_W), ("a_z", SWA_W),
    ("m_q", M_QK_W), ("m_k", M_QK_W), ("m_v", M_V_W), ("m_i", M_HEADS), ("m_f", M_HEADS),
    ("m_o", M_V_W), ("m_z", M_V_W), ("c_q", X_W), ("c_z", X_W),
)
_PROJ_ORDER = ("a_q", "a_z", "a_k", "a_v", "m_q", "m_k", "m_v", "m_o", "m_z", "c_q", "c_z")
GATE_W = 128
V7X_VMEM_LIMIT = 56 * 1024 * 1024


def _ref_offsets():
    offs, acc = {}, 0
    for name, width in _REF_SPLITS:
        offs[name] = (acc, width)
        acc += width
    return offs


def _proj_offsets():
    widths = dict(_REF_SPLITS)
    offs, acc = {}, 0
    for name in _PROJ_ORDER:
        offs[name] = (acc, widths[name])
        acc += widths[name]
    return offs, acc


_PROJ_OFFS, PROJ_W = _proj_offsets()


def _col_block(name, width):
    off, _ = _PROJ_OFFS[name]
    assert off % width == 0
    return off // width


def _silu(z):
    return z * (1.0 / (1.0 + jnp.exp(-z)))


def _sigmoid(z):
    return 1.0 / (1.0 + jnp.exp(-z))


def _log_sigmoid(z):
    return jnp.minimum(z, 0.0) - jnp.log1p(jnp.exp(-jnp.abs(z)))


def _in_proj_kernel(x_ref, w_ref, wg_ref, o_ref, g_ref):
    o_ref[...] = jnp.dot(x_ref[...], w_ref[...], preferred_element_type=F32).astype(o_ref.dtype)

    @pl.when(pl.program_id(1) == 0)
    def _():
        g_ref[...] = jnp.dot(x_ref[...], wg_ref[...], preferred_element_type=F32)


def _in_proj(x_bf, w_main, w_gate, *, tm=1024, tn=1024):
    T, K = x_bf.shape
    N = w_main.shape[1]
    return pl.pallas_call(
        _in_proj_kernel,
        out_shape=(jax.ShapeDtypeStruct((T, N), BF16), jax.ShapeDtypeStruct((T, GATE_W), F32)),
        grid=(T // tm, N // tn),
        in_specs=[
            pl.BlockSpec((tm, K), lambda i, j: (i, 0)),
            pl.BlockSpec((K, tn), lambda i, j: (0, j)),
            pl.BlockSpec((K, GATE_W), lambda i, j: (0, 0)),
        ],
        out_specs=(
            pl.BlockSpec((tm, tn), lambda i, j: (i, j)),
            pl.BlockSpec((tm, GATE_W), lambda i, j: (i, 0)),
        ),
        compiler_params=pltpu.CompilerParams(
            dimension_semantics=("arbitrary", "arbitrary"),
            vmem_limit_bytes=V7X_VMEM_LIMIT),
        name="in_proj",
    )(x_bf, w_main, w_gate)


def _swa_kernel(sink_ref, q_ref, z_ref, kp_ref, kc_ref, vp_ref, vc_ref, bias_ref, o_ref):
    scale = SWA_HEAD_DIM ** -0.5
    for kv in range(SWA_KV_HEADS):
        ksl = slice(kv * SWA_HEAD_DIM, (kv + 1) * SWA_HEAD_DIM)
        kk = jnp.concatenate([kp_ref[:, ksl], kc_ref[:, ksl]], axis=0)
        vv = jnp.concatenate([vp_ref[:, ksl], vc_ref[:, ksl]], axis=0)
        heads = [kv * SWA_GROUP + g for g in range(SWA_GROUP)]
        q4 = jnp.concatenate(
            [q_ref[:, h * SWA_HEAD_DIM:(h + 1) * SWA_HEAD_DIM] for h in heads], axis=0)
        s4 = lax.dot_general(q4, kk, (((1,), (1,)), ((), ())), preferred_element_type=F32)
        ps, denoms = [], []
        for g, h in enumerate(heads):
            s = s4[g * BLOCK:(g + 1) * BLOCK] * scale + bias_ref[0, h]
            sink = sink_ref[h]
            m = jnp.maximum(jnp.max(s, axis=-1, keepdims=True), sink)
            p = jnp.exp(s - m)
            denoms.append(jnp.sum(p, axis=-1, keepdims=True) + jnp.exp(sink - m))
            ps.append(p.astype(BF16))
        o4 = jnp.dot(jnp.concatenate(ps, axis=0), vv, preferred_element_type=F32)
        for g, h in enumerate(heads):
            hsl = slice(h * SWA_HEAD_DIM, (h + 1) * SWA_HEAD_DIM)
            o = o4[g * BLOCK:(g + 1) * BLOCK] / denoms[g]
            o_ref[:, hsl] = (o * _silu(z_ref[:, hsl].astype(F32))).astype(o_ref.dtype)


def _swa(proj, bias_tbl, sinks, *, batch, seq):
    T = proj.shape[0]
    nb = seq // BLOCK
    qc = _col_block("a_q", SWA_W)
    zc = _col_block("a_z", SWA_W)
    kc = _col_block("a_k", SWA_KV_W)
    vc = _col_block("a_v", SWA_KV_W)

    def cur(col):
        return lambda b, n, *_: (b * nb + n, col)

    def prev(col):
        return lambda b, n, *_: (b * nb + jnp.maximum(n - 1, 0), col)

    grid_spec = pltpu.PrefetchScalarGridSpec(
        num_scalar_prefetch=1,
        grid=(batch, nb),
        in_specs=[
            pl.BlockSpec((BLOCK, SWA_W), cur(qc)),
            pl.BlockSpec((BLOCK, SWA_W), cur(zc)),
            pl.BlockSpec((BLOCK, SWA_KV_W), prev(kc)),
            pl.BlockSpec((BLOCK, SWA_KV_W), cur(kc)),
            pl.BlockSpec((BLOCK, SWA_KV_W), prev(vc)),
            pl.BlockSpec((BLOCK, SWA_KV_W), cur(vc)),
            pl.BlockSpec((1, SWA_HEADS, BLOCK, 2 * BLOCK),
                         lambda b, n, *_: (jnp.minimum(n, 1), 0, 0, 0)),
        ],
        out_specs=pl.BlockSpec((BLOCK, SWA_W), lambda b, n, *_: (b * nb + n, 0)),
    )
    return pl.pallas_call(
        _swa_kernel,
        out_shape=jax.ShapeDtypeStruct((T, SWA_W), BF16),
        grid_spec=grid_spec,
        compiler_params=pltpu.CompilerParams(dimension_semantics=("arbitrary", "arbitrary")),
        name="swa",
    )(sinks, proj, proj, proj, proj, proj, proj, bias_tbl)


def _t5_bucket_np(dist):
    max_exact = NUM_BUCKETS // 2
    is_small = dist < max_exact
    ratio = np.maximum(dist, max_exact).astype(np.float32) / np.float32(max_exact)
    large = max_exact + (np.log(ratio) / np.float32(math.log(MAX_DISTANCE / max_exact))
                         * np.float32(NUM_BUCKETS - max_exact)).astype(np.int32)
    large = np.minimum(large, NUM_BUCKETS - 1)
    return np.where(is_small, dist, large)


def _swa_bias_table(rel_bias):
    qi = np.arange(BLOCK, dtype=np.int32)[:, None]
    si = np.arange(2 * BLOCK, dtype=np.int32)[None, :]
    dist = qi + BLOCK - si
    in_win = (dist >= 0) & (dist < WINDOW)
    valid = np.stack([in_win & (si >= BLOCK), in_win], axis=0)
    bucket = _t5_bucket_np(np.clip(dist, 0, MAX_DISTANCE - 1))
    bias = rel_bias.astype(F32)[bucket].transpose(2, 0, 1)
    return jnp.where(valid[:, None], bias[None], -jnp.inf)


def _cumsum_lanes(x):
    n = x.shape[-1]
    lane = lax.broadcasted_iota(jnp.int32, x.shape, x.ndim - 1)
    shift = 1
    while shift < n:
        x = x + jnp.where(lane >= shift, pltpu.roll(x, shift, x.ndim - 1), 0.0)
        shift *= 2
    return x


def _mlstm_kernel(q_ref, k_ref, v_ref, g_ref, o_ref, z_ref, cw_ref, cb_ref, gb_ref, ng_ref,
                  out_ref, xbuf, c_state, n_state, m_state):
    L = M_CHUNK
    HALO = 8

    @pl.when(pl.program_id(1) == 0)
    def _():
        xbuf[0:HALO, :] = jnp.zeros((HALO, 2 * M_QK_W), F32)
        c_state[...] = jnp.zeros_like(c_state)
        n_state[...] = jnp.zeros_like(n_state)
        m_state[...] = jnp.zeros_like(m_state)

    xbuf[HALO:HALO + L, 0:M_QK_W] = q_ref[...].astype(F32)
    xbuf[HALO:HALO + L, M_QK_W:] = k_ref[...].astype(F32)
    acc = cb_ref[...] + cw_ref[CONV_WIDTH - 1:CONV_WIDTH, :] * xbuf[HALO:HALO + L, :]
    for j in range(CONV_WIDTH - 1):
        back = CONV_WIDTH - 1 - j
        acc = acc + cw_ref[j:j + 1, :] * xbuf[HALO - back:HALO - back + L, :]
    tail = xbuf[L:L + HALO, :]
    xbuf[0:HALO, :] = tail
    qk = _silu(acc)
    q_all = qk[:, 0:M_QK_W]
    k_all = qk[:, M_QK_W:] * (M_QK_DIM ** -0.5)

    gt = (g_ref[...] + gb_ref[...]).T[0:8, :]
    row8 = lax.broadcasted_iota(jnp.int32, (8, L), 0)
    head_rows = row8 >= M_HEADS
    log_i = jnp.where(head_rows, pltpu.roll(gt, M_HEADS, 0), 0.0)
    log_f = jnp.where(head_rows, _log_sigmoid(gt), 0.0)
    b = _cumsum_lanes(log_f)
    g_tot = b[:, L - 1:L]
    a = g_tot - b + log_i
    m_loc = jnp.max(a, axis=-1, keepdims=True)
    w = jnp.exp(a - m_loc)
    m_prev = m_state[...]
    inter = b + m_prev
    r = log_i - b

    cols = jnp.concatenate([inter, w, b, jnp.zeros((L - 24, L), F32)], axis=0).T

    g_rep = jnp.broadcast_to(g_tot, (8, L))
    m_loc_rep = jnp.broadcast_to(m_loc, (8, L))
    m_new = jnp.maximum(g_rep + m_prev, m_loc_rep)
    s_prev = jnp.exp(g_rep + m_prev - m_new)
    s_loc = jnp.exp(m_loc_rep - m_new)
    n_prev_all = n_state[...]

    ti = lax.broadcasted_iota(jnp.int32, (L, L), 0)
    si = lax.broadcasted_iota(jnp.int32, (L, L), 1)
    causal = ti >= si

    for h in range(M_HEADS):
        rr = M_HEADS + h
        qh = q_all[:, h * M_QK_DIM:(h + 1) * M_QK_DIM]
        kh = k_all[:, h * M_QK_DIM:(h + 1) * M_QK_DIM]
        vh = v_ref[:, h * M_V_DIM:(h + 1) * M_V_DIM]
        inter_c = cols[:, rr:rr + 1]
        w_c = cols[:, 8 + rr:9 + rr]
        b_c = cols[:, 16 + rr:17 + rr]

        dmat = jnp.where(causal, b_c + r[rr:rr + 1, :], -jnp.inf)
        m_t = jnp.maximum(inter_c, jnp.max(dmat, axis=-1, keepdims=True))
        qb = qh.astype(BF16)
        kb = kh.astype(BF16)
        s = lax.dot_general(qb, kb, (((1,), (1,)), ((), ())), preferred_element_type=F32)
        sm = s * jnp.exp(dmat - m_t)
        w_inter = jnp.exp(inter_c - m_t)
        c_prev = c_state[h]
        num = (jnp.dot(sm.astype(BF16), vh, preferred_element_type=F32)
               + w_inter * jnp.dot(qb, c_prev.astype(BF16), preferred_element_type=F32))
        n_prev = n_prev_all[rr:rr + 1, :]
        den = (jnp.sum(sm, axis=-1, keepdims=True)
               + w_inter * jnp.sum(qh * n_prev, axis=-1, keepdims=True))
        hcell = num / jnp.maximum(jnp.abs(den), jnp.exp(-m_t))

        vsl = slice(h * M_V_DIM, (h + 1) * M_V_DIM)
        hg = _sigmoid(o_ref[:, vsl].astype(F32)) * hcell
        mu = jnp.mean(hg, axis=-1, keepdims=True)
        var = jnp.mean(jnp.square(hg - mu), axis=-1, keepdims=True)
        hn = (hg - mu) * lax.rsqrt(var + HEAD_NORM_EPS) * ng_ref[:, vsl]
        out_ref[:, vsl] = (hn * _silu(z_ref[:, vsl].astype(F32))).astype(out_ref.dtype)

        wk = w_c * kh
        c_loc = jnp.dot(wk.T.astype(BF16), vh, preferred_element_type=F32)
        sp_row = s_prev[rr:rr + 1, :]
        sl_row = s_loc[rr:rr + 1, :]
        sp2 = jnp.concatenate([sp_row, sp_row], axis=1)
        sl2 = jnp.concatenate([sl_row, sl_row], axis=1)
        c_state[h] = sp2 * c_prev + sl2 * c_loc
        n_loc = jnp.sum(wk, axis=0, keepdims=True)
        n_state[rr:rr + 1, :] = sp_row * n_prev + sl_row * n_loc

    m_state[...] = m_new


def _mlstm(proj, gates, conv_w, conv_b, gate_bias, norm_g, *, batch, seq):
    T = proj.shape[0]
    L = M_CHUNK
    nc = seq // L
    assert M_QK_DIM == L, "lane-replicated state rows assume chunk length == qk head dim"

    def tok(col):
        return lambda b, c: (b * nc + c, col)

    const = lambda b, c: (0, 0)
    return pl.pallas_call(
        _mlstm_kernel,
        out_shape=jax.ShapeDtypeStruct((T, M_V_W), BF16),
        grid=(batch, nc),
        in_specs=[
            pl.BlockSpec((L, M_QK_W), tok(_col_block("m_q", M_QK_W))),
            pl.BlockSpec((L, M_QK_W), tok(_col_block("m_k", M_QK_W))),
            pl.BlockSpec((L, M_V_W), tok(_col_block("m_v", M_V_W))),
            pl.BlockSpec((L, GATE_W), tok(0)),
            pl.BlockSpec((L, M_V_W), tok(_col_block("m_o", M_V_W))),
            pl.BlockSpec((L, M_V_W), tok(_col_block("m_z", M_V_W))),
            pl.BlockSpec((CONV_WIDTH, 2 * M_QK_W), const),
            pl.BlockSpec((1, 2 * M_QK_W), const),
            pl.BlockSpec((1, GATE_W), const),
            pl.BlockSpec((1, M_V_W), const),
        ],
        out_specs=pl.BlockSpec((L, M_V_W), tok(0)),
        scratch_shapes=[
            pltpu.VMEM((L + 8, 2 * M_QK_W), F32),
            pltpu.VMEM((M_HEADS, M_QK_DIM, M_V_DIM), F32),
            pltpu.VMEM((8, M_QK_DIM), F32),
            pltpu.VMEM((8, L), F32),
        ],
        compiler_params=pltpu.CompilerParams(dimension_semantics=("arbitrary", "arbitrary")),
        name="mlstm",
    )(proj, proj, proj, gates, proj, proj, conv_w, conv_b, gate_bias, norm_g)


def _mem_kv_kernel(mem_ref, w_ref, o_ref):
    o_ref[...] = jnp.dot(mem_ref[...].astype(BF16), w_ref[...].astype(BF16),
                         preferred_element_type=F32).astype(o_ref.dtype)


def _mem_kv(mem2d, w_mem_kv, *, tn=512):
    M, K = mem2d.shape
    N = w_mem_kv.shape[1]
    return pl.pallas_call(
        _mem_kv_kernel,
        out_shape=jax.ShapeDtypeStruct((M, N), BF16),
        grid=(N // tn,),
        in_specs=[pl.BlockSpec((M, K), lambda j: (0, 0)),
                  pl.BlockSpec((K, tn), lambda j: (0, j))],
        out_specs=pl.BlockSpec((M, tn), lambda j: (0, j)),
        compiler_params=pltpu.CompilerParams(
            dimension_semantics=("arbitrary",), vmem_limit_bytes=V7X_VMEM_LIMIT),
        name="mem_kv",
    )(mem2d, w_mem_kv)


def _xattn_kernel(q_ref, z_ref, k_ref, v_ref, o_ref):
    scale = X_HEAD_DIM ** -0.5
    for h in range(X_HEADS):
        hsl = slice(h * X_HEAD_DIM, (h + 1) * X_HEAD_DIM)
        s = lax.dot_general(q_ref[:, hsl], k_ref[:, hsl], (((1,), (1,)), ((), ())),
                            preferred_element_type=F32) * scale
        m = jnp.max(s, axis=-1, keepdims=True)
        p = jnp.exp(s - m)
        denom = jnp.sum(p, axis=-1, keepdims=True)
        o = jnp.dot(p.astype(BF16), v_ref[:, hsl], preferred_element_type=F32) / denom
        o_ref[:, hsl] = (o * _silu(z_ref[:, hsl].astype(F32))).astype(o_ref.dtype)


def _xattn(proj, mkv, *, batch, seq, tq=512):
    T = proj.shape[0]
    nq = seq // tq
    tok = lambda col: (lambda b, i: (b * nq + i, col))
    return pl.pallas_call(
        _xattn_kernel,
        out_shape=jax.ShapeDtypeStruct((T, X_W), BF16),
        grid=(batch, nq),
        in_specs=[
            pl.BlockSpec((tq, X_W), tok(_col_block("c_q", X_W))),
            pl.BlockSpec((tq, X_W), tok(_col_block("c_z", X_W))),
            pl.BlockSpec((MEM_LEN, X_W), lambda b, i: (b, 0)),
            pl.BlockSpec((MEM_LEN, X_W), lambda b, i: (b, 1)),
        ],
        out_specs=pl.BlockSpec((tq, X_W), tok(0)),
        compiler_params=pltpu.CompilerParams(dimension_semantics=("arbitrary", "arbitrary")),
        name="xattn",
    )(proj, proj, mkv, mkv)


def _out_proj_kernel(a_ref, m_ref, c_ref, wa_ref, wm_ref, wc_ref, x_ref, g_ref, b_ref, o_ref, *, tn):
    j = pl.program_id(1)
    y = (jnp.dot(a_ref[...], wa_ref[...], preferred_element_type=F32)
         + jnp.dot(m_ref[...], wm_ref[...], preferred_element_type=F32)
         + jnp.dot(c_ref[...], wc_ref[...], preferred_element_type=F32))
    col = pl.multiple_of(j * tn, tn)
    o_ref[:, pl.ds(col, tn)] = DEEPNORM_ALPHA * x_ref[...] + y

    @pl.when(j == pl.num_programs(1) - 1)
    def _():
        r = o_ref[...]
        mu = jnp.mean(r, axis=-1, keepdims=True)
        d = r - mu
        var = jnp.mean(jnp.square(d), axis=-1, keepdims=True)
        o_ref[...] = d * lax.rsqrt(var + LN_EPS) * g_ref[...] + b_ref[...]


def _out_proj(a_out, m_out, c_out, w_out_bf, x2d, ln_g, ln_b, *, tm=512, tn=1024):
    T, N = x2d.shape
    return pl.pallas_call(
        functools.partial(_out_proj_kernel, tn=tn),
        out_shape=jax.ShapeDtypeStruct((T, N), F32),
        grid=(T // tm, N // tn),
        in_specs=[
            pl.BlockSpec((tm, SWA_W), lambda i, j: (i, 0)),
            pl.BlockSpec((tm, M_V_W), lambda i, j: (i, 0)),
            pl.BlockSpec((tm, X_W), lambda i, j: (i, 0)),
            pl.BlockSpec((SWA_W, tn), lambda i, j: (0, j)),
            pl.BlockSpec((M_V_W, tn), lambda i, j: (SWA_W // M_V_W, j)),
            pl.BlockSpec((X_W, tn), lambda i, j: ((SWA_W + M_V_W) // X_W, j)),
            pl.BlockSpec((tm, tn), lambda i, j: (i, j)),
            pl.BlockSpec((1, N), lambda i, j: (0, 0)),
            pl.BlockSpec((1, N), lambda i, j: (0, 0)),
        ],
        out_specs=pl.BlockSpec((tm, N), lambda i, j: (i, 0)),
        compiler_params=pltpu.CompilerParams(
            dimension_semantics=("arbitrary", "arbitrary"), vmem_limit_bytes=V7X_VMEM_LIMIT),
        name="out_proj",
    )(a_out, m_out, c_out, w_out_bf, w_out_bf, w_out_bf, x2d, ln_g, ln_b)


def kernel(x, mem, w_in, conv_w, conv_b, b_i, b_f, m_norm_g, rel_bias, sinks, w_mem_kv, w_out, ln_g, ln_b):
    B, S, D = x.shape
    assert D == D_MODEL and S % BLOCK == 0 and S % M_CHUNK == 0
    T = B * S
    x2d = x.reshape(T, D)

    ref_offs = _ref_offsets()
    seg = lambda name: w_in[:, ref_offs[name][0]:ref_offs[name][0] + ref_offs[name][1]]
    w_main = jnp.concatenate([seg(n) for n in _PROJ_ORDER], axis=1).astype(BF16)
    w_gate = jnp.concatenate(
        [seg("m_i"), seg("m_f"), jnp.zeros((D, GATE_W - 2 * M_HEADS), w_in.dtype)], axis=1).astype(BF16)
    gate_bias = jnp.concatenate(
        [b_i.astype(F32), b_f.astype(F32), jnp.zeros((GATE_W - 2 * M_HEADS,), F32)])[None, :]

    proj, gates = _in_proj(x2d.astype(BF16), w_main, w_gate)

    a_out = _swa(proj, _swa_bias_table(rel_bias), sinks.astype(F32), batch=B, seq=S)
    m_out = _mlstm(proj, gates, conv_w.astype(F32), conv_b.astype(F32)[None, :], gate_bias,
                   m_norm_g.astype(F32)[None, :], batch=B, seq=S)
    mkv = _mem_kv(mem.reshape(B * MEM_LEN, D), w_mem_kv)
    c_out = _xattn(proj, mkv, batch=B, seq=S)

    out = _out_proj(a_out, m_out, c_out, w_out.astype(BF16), x2d,
                    ln_g.astype(F32)[None, :], ln_b.astype(F32)[None, :])
    return out.reshape(B, S, D).astype(x.dtype)
```

```python
import functools
import math

import jax
import jax.numpy as jnp
import numpy as np
from jax import lax
from jax.experimental import pallas as pl
from jax.experimental.pallas import tpu as pltpu

F32 = jnp.float32
BF16 = jnp.bfloat16

D_MODEL = 4096
MEM_LEN = 256
SWA_HEADS = 16
SWA_KV_HEADS = 4
SWA_GROUP = SWA_HEADS // SWA_KV_HEADS
SWA_HEAD_DIM = 128
WINDOW = 128
BLOCK = 128
M_HEADS = 4
M_QK_DIM = 128
M_V_DIM = 256
M_CHUNK = 128
CONV_WIDTH = 4
X_HEADS = 4
X_HEAD_DIM = 256
NUM_BUCKETS = 32
MAX_DISTANCE = 128

SWA_W = SWA_HEADS * SWA_HEAD_DIM
SWA_KV_W = SWA_KV_HEADS * SWA_HEAD_DIM
M_QK_W = M_HEADS * M_QK_DIM
M_V_W = M_HEADS * M_V_DIM
X_W = X_HEADS * X_HEAD_DIM
MIX_W = SWA_W + M_V_W + X_W

DEEPNORM_ALPHA = 2.0 ** 0.25
LN_EPS = 1e-5
HEAD_NORM_EPS = 1e-6

_REF_SPLITS = (
    ("a_q", SWA_W), ("a_k", SWA_KV_W), ("a_v", SWA_KV_W), ("a_z", SWA_W),
    ("m_q", M_QK_W), ("m_k", M_QK_W), ("m_v", M_V_W), ("m_i", M_HEADS), ("m_f", M_HEADS),
    ("m_o", M_V_W), ("m_z", M_V_W), ("c_q", X_W), ("c_z", X_W),
)
GATE_W = 128
V7X_VMEM_LIMIT = 56 * 1024 * 1024


def _ref_offsets():
    offs, acc = {}, 0
    for name, width in _REF_SPLITS:
        offs[name] = acc
        acc += width
    return offs


_REF_OFFS = _ref_offsets()
HEAD_W = _REF_OFFS["m_i"]
GATE_COLS = 2 * M_HEADS
TAIL_START = HEAD_W + GATE_COLS
TAIL_W = 2 * M_V_W + 2 * X_W
PROJ_W = HEAD_W + TAIL_W
_HEAD_SEGS = ("a_q", "a_k", "a_v", "a_z", "m_q", "m_k", "m_v")


def _col_block(name, width):
    off = _REF_OFFS[name] - (0 if name in _HEAD_SEGS else GATE_COLS)
    assert off % width == 0, (name, off, width)
    return off // width


def _silu(z):
    return z * (1.0 / (1.0 + jnp.exp(-z)))


def _sigmoid(z):
    return 1.0 / (1.0 + jnp.exp(-z))


def _log_sigmoid(z):
    return jnp.minimum(z, 0.0) - jnp.log1p(jnp.exp(-jnp.abs(z)))


_NT = (((1,), (1,)), ((), ()))


def _in_proj_kernel(x_ref, w_ref, wg_ref, o_ref, g_ref):
    w = w_ref[...].astype(BF16)
    o_ref[...] = lax.dot_general(x_ref[...], w, _NT, preferred_element_type=F32).astype(o_ref.dtype)

    @pl.when(pl.program_id(1) == 0)
    def _():
        g_ref[...] = lax.dot_general(x_ref[...], wg_ref[...].astype(BF16), _NT, preferred_element_type=F32)


def _in_proj(x_bf, w_t, *, tm=1024, tn=512):
    T, K = x_bf.shape
    n_head, n_tail = HEAD_W // tn, TAIL_W // tn
    assert HEAD_W % tn == 0 and TAIL_W % tn == 0 and HEAD_W % GATE_W == 0

    def w_row(i, j):
        row = jnp.where(j < n_head, j * tn, TAIL_START + (j - n_head) * tn)
        return pl.multiple_of(row, math.gcd(tn, TAIL_START)), 0

    return pl.pallas_call(
        _in_proj_kernel,
        out_shape=(jax.ShapeDtypeStruct((T, PROJ_W), BF16), jax.ShapeDtypeStruct((T, GATE_W), F32)),
        grid=(T // tm, n_head + n_tail),
        in_specs=[
            pl.BlockSpec((tm, K), lambda i, j: (i, 0)),
            pl.BlockSpec((pl.Element(tn), pl.Element(K)), w_row),
            pl.BlockSpec((GATE_W, K), lambda i, j: (HEAD_W // GATE_W, 0)),
        ],
        out_specs=(
            pl.BlockSpec((tm, tn), lambda i, j: (i, j)),
            pl.BlockSpec((tm, GATE_W), lambda i, j: (i, 0)),
        ),
        compiler_params=pltpu.CompilerParams(
            dimension_semantics=("arbitrary", "arbitrary"),
            vmem_limit_bytes=V7X_VMEM_LIMIT),
        name="in_proj",
    )(x_bf, w_t, w_t)


def _t5_bucket_np(dist):
    max_exact = NUM_BUCKETS // 2
    is_small = dist < max_exact
    ratio = np.maximum(dist, max_exact).astype(np.float32) / np.float32(max_exact)
    large = max_exact + (np.log(ratio) / np.float32(math.log(MAX_DISTANCE / max_exact))
                         * np.float32(NUM_BUCKETS - max_exact)).astype(np.int32)
    large = np.minimum(large, NUM_BUCKETS - 1)
    return np.where(is_small, dist, large)


def _swa_bucket_row():
    c = np.arange(2 * BLOCK, dtype=np.int32)
    return _t5_bucket_np(np.clip(BLOCK - c, 0, MAX_DISTANCE - 1)).astype(np.int32)[None, :]


def _swa_build_bias(bucket_ref, relb_ref, tbl_ref):
    q = lax.broadcasted_iota(jnp.int32, (BLOCK, 2 * BLOCK), 0)
    s = lax.broadcasted_iota(jnp.int32, (BLOCK, 2 * BLOCK), 1)
    dist = q + BLOCK - s
    in_win = jnp.where(dist >= 0, jnp.where(dist < WINDOW, 1, 0), 0)
    in_win_first = jnp.where(s >= BLOCK, in_win, 0)
    bucket = bucket_ref[...]
    for h in range(SWA_HEADS):
        u = jnp.zeros((1, 2 * BLOCK), F32)
        for bkt in range(NUM_BUCKETS):
            u = jnp.where(bucket == bkt, relb_ref[bkt * SWA_HEADS + h], u)
        t = pltpu.roll(jnp.broadcast_to(u, (BLOCK, 2 * BLOCK)), 0, 1, stride=1, stride_axis=0)
        tbl_ref[1, h] = jnp.where(in_win == 1, t, -jnp.inf)
        tbl_ref[0, h] = jnp.where(in_win_first == 1, t, -jnp.inf)


def _swa_kernel(sink_ref, relb_ref, bucket_ref, q_ref, z0_ref, z1_ref, kp_ref, kc_ref, vp_ref, vc_ref,
                o_ref, tbl_ref, *, nblk):
    scale = SWA_HEAD_DIM ** -0.5
    b, n = pl.program_id(0), pl.program_id(1)

    @pl.when((b == 0) & (n == 0))
    def _():
        _swa_build_bias(bucket_ref, relb_ref, tbl_ref)

    half = SWA_HEADS // 2
    for r in range(nblk):
        rows = slice(r * BLOCK, (r + 1) * BLOCK)
        first = jnp.minimum(n, 1) if r == 0 else 1
        for kv in range(SWA_KV_HEADS):
            ksl = slice(kv * SWA_HEAD_DIM, (kv + 1) * SWA_HEAD_DIM)
            if r == 0:
                kk = jnp.concatenate([kp_ref[:, ksl], kc_ref[rows, ksl]], axis=0)
                vv = jnp.concatenate([vp_ref[:, ksl], vc_ref[rows, ksl]], axis=0)
            else:
                kk = kc_ref[(r - 1) * BLOCK:(r + 1) * BLOCK, ksl]
                vv = vc_ref[(r - 1) * BLOCK:(r + 1) * BLOCK, ksl]
            heads = [kv * SWA_GROUP + g for g in range(SWA_GROUP)]
            q4 = jnp.concatenate(
                [q_ref[rows, h * SWA_HEAD_DIM:(h + 1) * SWA_HEAD_DIM] for h in heads], axis=0)
            s4 = lax.dot_general(q4, kk, _NT, preferred_element_type=F32)
            ps, denoms = [], []
            for g, h in enumerate(heads):
                s = s4[g * BLOCK:(g + 1) * BLOCK] * scale + tbl_ref[first, h]
                sink = sink_ref[h]
                m = jnp.maximum(jnp.max(s, axis=-1, keepdims=True), sink)
                p = jnp.exp(s - m)
                denoms.append(jnp.sum(p, axis=-1, keepdims=True) + jnp.exp(sink - m))
                ps.append(p.astype(BF16))
            o4 = jnp.dot(jnp.concatenate(ps, axis=0), vv, preferred_element_type=F32)
            for g, h in enumerate(heads):
                z_ref = z0_ref if h < half else z1_ref
                zsl = slice((h % half) * SWA_HEAD_DIM, (h % half + 1) * SWA_HEAD_DIM)
                o = o4[g * BLOCK:(g + 1) * BLOCK] / denoms[g]
                o_ref[rows, h * SWA_HEAD_DIM:(h + 1) * SWA_HEAD_DIM] = (
                    o * _silu(z_ref[rows, zsl].astype(F32))).astype(o_ref.dtype)


def _swa(proj, rel_bias, sinks, *, batch, seq, nblk=4):
    T = proj.shape[0]
    tq = nblk * BLOCK
    nt = seq // tq
    assert seq % tq == 0
    qc = _col_block("a_q", SWA_W)
    zc = _col_block("a_z", SWA_W // 2)
    kc = _col_block("a_k", SWA_KV_W)
    vc = _col_block("a_v", SWA_KV_W)

    def cur(col):
        return lambda b, n, *_: (b * nt + n, col)

    def prev(col):
        return lambda b, n, *_: (jnp.maximum((b * nt + n) * nblk - 1, 0), col)

    grid_spec = pltpu.PrefetchScalarGridSpec(
        num_scalar_prefetch=2,
        grid=(batch, nt),
        in_specs=[
            pl.BlockSpec((1, 2 * BLOCK), lambda b, n, *_: (0, 0)),
            pl.BlockSpec((tq, SWA_W), cur(qc)),
            pl.BlockSpec((tq, SWA_W // 2), cur(zc)),
            pl.BlockSpec((tq, SWA_W // 2), cur(zc + 1)),
            pl.BlockSpec((BLOCK, SWA_KV_W), prev(kc)),
            pl.BlockSpec((tq, SWA_KV_W), cur(kc)),
            pl.BlockSpec((BLOCK, SWA_KV_W), prev(vc)),
            pl.BlockSpec((tq, SWA_KV_W), cur(vc)),
        ],
        out_specs=pl.BlockSpec((tq, SWA_W), lambda b, n, *_: (b * nt + n, 0)),
        scratch_shapes=[pltpu.VMEM((2, SWA_HEADS, BLOCK, 2 * BLOCK), F32)],
    )
    return pl.pallas_call(
        functools.partial(_swa_kernel, nblk=nblk),
        out_shape=jax.ShapeDtypeStruct((T, SWA_W), BF16),
        grid_spec=grid_spec,
        compiler_params=pltpu.CompilerParams(
            dimension_semantics=("arbitrary", "arbitrary"), vmem_limit_bytes=V7X_VMEM_LIMIT),
        name="swa",
    )(sinks, rel_bias.reshape(-1), jnp.asarray(_swa_bucket_row()), proj, proj, proj, proj, proj, proj, proj)


def _cumsum_lanes(x):
    n = x.shape[-1]
    lane = lax.broadcasted_iota(jnp.int32, x.shape, x.ndim - 1)
    shift = 1
    while shift < n:
        x = x + jnp.where(lane >= shift, pltpu.roll(x, shift, x.ndim - 1), 0.0)
        shift *= 2
    return x


def _mlstm_kernel(q_ref, k_ref, v_ref, g_ref, o_ref, z_ref, cw_ref, cb_ref, gb_ref, ng_ref,
                  out_ref, xbuf, c_state, n_state, m_state):
    L = M_CHUNK
    HALO = 8

    @pl.when(pl.program_id(1) == 0)
    def _():
        xbuf[0:HALO, :] = jnp.zeros((HALO, 2 * M_QK_W), F32)
        c_state[...] = jnp.zeros_like(c_state)
        n_state[...] = jnp.zeros_like(n_state)
        m_state[...] = jnp.zeros_like(m_state)

    xbuf[HALO:HALO + L, 0:M_QK_W] = q_ref[...].astype(F32)
    xbuf[HALO:HALO + L, M_QK_W:] = k_ref[...].astype(F32)
    acc = cb_ref[...] + cw_ref[CONV_WIDTH - 1:CONV_WIDTH, :] * xbuf[HALO:HALO + L, :]
    for j in range(CONV_WIDTH - 1):
        back = CONV_WIDTH - 1 - j
        acc = acc + cw_ref[j:j + 1, :] * xbuf[HALO - back:HALO - back + L, :]
    tail = xbuf[L:L + HALO, :]
    xbuf[0:HALO, :] = tail
    qk = _silu(acc)
    q_all = qk[:, 0:M_QK_W]
    k_all = qk[:, M_QK_W:] * (M_QK_DIM ** -0.5)

    gt = (g_ref[...] + gb_ref[...]).T[0:8, :]
    row8 = lax.broadcasted_iota(jnp.int32, (8, L), 0)
    head_rows = row8 >= M_HEADS
    log_i = jnp.where(head_rows, pltpu.roll(gt, M_HEADS, 0), 0.0)
    log_f = jnp.where(head_rows, _log_sigmoid(gt), 0.0)
    b = _cumsum_lanes(log_f)
    g_tot = b[:, L - 1:L]
    a = g_tot - b + log_i
    m_loc = jnp.max(a, axis=-1, keepdims=True)
    w = jnp.exp(a - m_loc)
    m_prev = m_state[...]
    inter = b + m_prev
    r = log_i - b

    cols = jnp.concatenate([inter, w, b, jnp.zeros((L - 24, L), F32)], axis=0).T

    g_rep = jnp.broadcast_to(g_tot, (8, L))
    m_loc_rep = jnp.broadcast_to(m_loc, (8, L))
    m_new = jnp.maximum(g_rep + m_prev, m_loc_rep)
    s_prev = jnp.exp(g_rep + m_prev - m_new)
    s_loc = jnp.exp(m_loc_rep - m_new)
    n_prev_all = n_state[...]

    ti = lax.broadcasted_iota(jnp.int32, (L, L), 0)
    si = lax.broadcasted_iota(jnp.int32, (L, L), 1)
    causal = ti >= si

    for h in range(M_HEADS):
        rr = M_HEADS + h
        qh = q_all[:, h * M_QK_DIM:(h + 1) * M_QK_DIM]
        kh = k_all[:, h * M_QK_DIM:(h + 1) * M_QK_DIM]
        vh = v_ref[:, h * M_V_DIM:(h + 1) * M_V_DIM]
        inter_c = cols[:, rr:rr + 1]
        w_c = cols[:, 8 + rr:9 + rr]
        b_c = cols[:, 16 + rr:17 + rr]

        dmat = jnp.where(causal, b_c + r[rr:rr + 1, :], -jnp.inf)
        m_t = jnp.maximum(inter_c, jnp.max(dmat, axis=-1, keepdims=True))
        qb = qh.astype(BF16)
        kb = kh.astype(BF16)
        s = lax.dot_general(qb, kb, _NT, preferred_element_type=F32)
        sm = s * jnp.exp(dmat - m_t)
        w_inter = jnp.exp(inter_c - m_t)
        c_prev = c_state[h]
        num = (jnp.dot(sm.astype(BF16), vh, preferred_element_type=F32)
               + w_inter * jnp.dot(qb, c_prev.astype(BF16), preferred_element_type=F32))
        n_prev = n_prev_all[rr:rr + 1, :]
        den = (jnp.sum(sm, axis=-1, keepdims=True)
               + w_inter * jnp.sum(qh * n_prev, axis=-1, keepdims=True))
        hcell = num / jnp.maximum(jnp.abs(den), jnp.exp(-m_t))

        vsl = slice(h * M_V_DIM, (h + 1) * M_V_DIM)
        hg = _sigmoid(o_ref[:, vsl].astype(F32)) * hcell
        mu = jnp.mean(hg, axis=-1, keepdims=True)
        var = jnp.mean(jnp.square(hg - mu), axis=-1, keepdims=True)
        hn = (hg - mu) * lax.rsqrt(var + HEAD_NORM_EPS) * ng_ref[:, vsl]
        out_ref[:, vsl] = (hn * _silu(z_ref[:, vsl].astype(F32))).astype(out_ref.dtype)

        wk = w_c * kh
        c_loc = jnp.dot(wk.T.astype(BF16), vh, preferred_element_type=F32)
        sp_row = s_prev[rr:rr + 1, :]
        sl_row = s_loc[rr:rr + 1, :]
        sp2 = jnp.concatenate([sp_row, sp_row], axis=1)
        sl2 = jnp.concatenate([sl_row, sl_row], axis=1)
        c_state[h] = sp2 * c_prev + sl2 * c_loc
        n_loc = jnp.sum(wk, axis=0, keepdims=True)
        n_state[rr:rr + 1, :] = sp_row * n_prev + sl_row * n_loc

    m_state[...] = m_new


def _mlstm(proj, gates, conv_w, conv_b, gate_bias, norm_g, *, batch, seq):
    T = proj.shape[0]
    L = M_CHUNK
    nc = seq // L
    assert M_QK_DIM == L, "lane-replicated state rows assume chunk length == qk head dim"

    def tok(col):
        return lambda b, c: (b * nc + c, col)

    const = lambda b, c: (0, 0)
    return pl.pallas_call(
        _mlstm_kernel,
        out_shape=jax.ShapeDtypeStruct((T, M_V_W), BF16),
        grid=(batch, nc),
        in_specs=[
            pl.BlockSpec((L, M_QK_W), tok(_col_block("m_q", M_QK_W))),
            pl.BlockSpec((L, M_QK_W), tok(_col_block("m_k", M_QK_W))),
            pl.BlockSpec((L, M_V_W), tok(_col_block("m_v", M_V_W))),
            pl.BlockSpec((L, GATE_W), tok(0)),
            pl.BlockSpec((L, M_V_W), tok(_col_block("m_o", M_V_W))),
            pl.BlockSpec((L, M_V_W), tok(_col_block("m_z", M_V_W))),
            pl.BlockSpec((CONV_WIDTH, 2 * M_QK_W), const),
            pl.BlockSpec((1, 2 * M_QK_W), const),
            pl.BlockSpec((1, GATE_W), const),
            pl.BlockSpec((1, M_V_W), const),
        ],
        out_specs=pl.BlockSpec((L, M_V_W), tok(0)),
        scratch_shapes=[
            pltpu.VMEM((L + 8, 2 * M_QK_W), F32),
            pltpu.VMEM((M_HEADS, M_QK_DIM, M_V_DIM), F32),
            pltpu.VMEM((8, M_QK_DIM), F32),
            pltpu.VMEM((8, L), F32),
        ],
        compiler_params=pltpu.CompilerParams(dimension_semantics=("arbitrary", "arbitrary")),
        name="mlstm",
    )(proj, proj, proj, gates, proj, proj, conv_w, conv_b, gate_bias, norm_g)


def _mem_kv_kernel(mem_ref, w_ref, o_ref):
    o_ref[...] = jnp.dot(mem_ref[...].astype(BF16), w_ref[...].astype(BF16),
                         preferred_element_type=F32).astype(o_ref.dtype)


def _mem_kv(mem2d, w_mem_kv, *, tn=512):
    M, K = mem2d.shape
    N = w_mem_kv.shape[1]
    return pl.pallas_call(
        _mem_kv_kernel,
        out_shape=jax.ShapeDtypeStruct((M, N), BF16),
        grid=(N // tn,),
        in_specs=[pl.BlockSpec((M, K), lambda j: (0, 0)),
                  pl.BlockSpec((K, tn), lambda j: (0, j))],
        out_specs=pl.BlockSpec((M, tn), lambda j: (0, j)),
        compiler_params=pltpu.CompilerParams(
            dimension_semantics=("arbitrary",), vmem_limit_bytes=V7X_VMEM_LIMIT),
        name="mem_kv",
    )(mem2d, w_mem_kv)


def _xattn_kernel(q_ref, z_ref, k_ref, v_ref, o_ref):
    scale = X_HEAD_DIM ** -0.5
    for h in range(X_HEADS):
        hsl = slice(h * X_HEAD_DIM, (h + 1) * X_HEAD_DIM)
        s = lax.dot_general(q_ref[:, hsl], k_ref[:, hsl], _NT,
                            preferred_element_type=F32) * scale
        m = jnp.max(s, axis=-1, keepdims=True)
        p = jnp.exp(s - m)
        denom = jnp.sum(p, axis=-1, keepdims=True)
        o = jnp.dot(p.astype(BF16), v_ref[:, hsl], preferred_element_type=F32) / denom
        o_ref[:, hsl] = (o * _silu(z_ref[:, hsl].astype(F32))).astype(o_ref.dtype)


def _xattn(proj, mkv, *, batch, seq, tq=512):
    T = proj.shape[0]
    nq = seq // tq
    tok = lambda col: (lambda b, i: (b * nq + i, col))
    return pl.pallas_call(
        _xattn_kernel,
        out_shape=jax.ShapeDtypeStruct((T, X_W), BF16),
        grid=(batch, nq),
        in_specs=[
            pl.BlockSpec((tq, X_W), tok(_col_block("c_q", X_W))),
            pl.BlockSpec((tq, X_W), tok(_col_block("c_z", X_W))),
            pl.BlockSpec((MEM_LEN, X_W), lambda b, i: (b, 0)),
            pl.BlockSpec((MEM_LEN, X_W), lambda b, i: (b, 1)),
        ],
        out_specs=pl.BlockSpec((tq, X_W), tok(0)),
        compiler_params=pltpu.CompilerParams(dimension_semantics=("arbitrary", "arbitrary")),
        name="xattn",
    )(proj, proj, mkv, mkv)


def _out_proj_kernel(a_ref, m_ref, c_ref, wa_ref, wm_ref, wc_ref, x_ref, g_ref, b_ref, o_ref, *, tn):
    j = pl.program_id(1)
    y = (jnp.dot(a_ref[...], wa_ref[...], preferred_element_type=F32)
         + jnp.dot(m_ref[...], wm_ref[...], preferred_element_type=F32)
         + jnp.dot(c_ref[...], wc_ref[...], preferred_element_type=F32))
    col = pl.multiple_of(j * tn, tn)
    o_ref[:, pl.ds(col, tn)] = DEEPNORM_ALPHA * x_ref[...] + y

    @pl.when(j == pl.num_programs(1) - 1)
    def _():
        r = o_ref[...]
        mu = jnp.mean(r, axis=-1, keepdims=True)
        d = r - mu
        var = jnp.mean(jnp.square(d), axis=-1, keepdims=True)
        o_ref[...] = d * lax.rsqrt(var + LN_EPS) * g_ref[...] + b_ref[...]


def _out_proj(a_out, m_out, c_out, w_out_bf, x2d, ln_g, ln_b, *, tm=512, tn=1024):
    T, N = x2d.shape
    return pl.pallas_call(
        functools.partial(_out_proj_kernel, tn=tn),
        out_shape=jax.ShapeDtypeStruct((T, N), F32),
        grid=(T // tm, N // tn),
        in_specs=[
            pl.BlockSpec((tm, SWA_W), lambda i, j: (i, 0)),
            pl.BlockSpec((tm, M_V_W), lambda i, j: (i, 0)),
            pl.BlockSpec((tm, X_W), lambda i, j: (i, 0)),
            pl.BlockSpec((SWA_W, tn), lambda i, j: (0, j)),
            pl.BlockSpec((M_V_W, tn), lambda i, j: (SWA_W // M_V_W, j)),
            pl.BlockSpec((X_W, tn), lambda i, j: ((SWA_W + M_V_W) // X_W, j)),
            pl.BlockSpec((tm, tn), lambda i, j: (i, j)),
            pl.BlockSpec((1, N), lambda i, j: (0, 0)),
            pl.BlockSpec((1, N), lambda i, j: (0, 0)),
        ],
        out_specs=pl.BlockSpec((tm, N), lambda i, j: (i, 0)),
        compiler_params=pltpu.CompilerParams(
            dimension_semantics=("arbitrary", "arbitrary"), vmem_limit_bytes=V7X_VMEM_LIMIT),
        name="out_proj",
    )(a_out, m_out, c_out, w_out_bf, w_out_bf, w_out_bf, x2d, ln_g, ln_b)


def kernel(x, mem, w_in, conv_w, conv_b, b_i, b_f, m_norm_g, rel_bias, sinks, w_mem_kv, w_out, ln_g, ln_b):
    B, S, D = x.shape
    assert D == D_MODEL and S % BLOCK == 0 and S % M_CHUNK == 0
    T = B * S
    x2d = x.reshape(T, D)
    gate_bias = jnp.concatenate(
        [b_i.astype(F32), b_f.astype(F32), jnp.zeros((GATE_W - GATE_COLS,), F32)])[None, :]

    proj, gates = _in_proj(x2d.astype(BF16), w_in.T)

    a_out = _swa(proj, rel_bias.astype(F32), sinks.astype(F32), batch=B, seq=S)
    m_out = _mlstm(proj, gates, conv_w.astype(F32), conv_b.astype(F32)[None, :], gate_bias,
                   m_norm_g.astype(F32)[None, :], batch=B, seq=S)
    mkv = _mem_kv(mem.reshape(B * MEM_LEN, D), w_mem_kv)
    c_out = _xattn(proj, mkv, batch=B, seq=S)

    out = _out_proj(a_out, m_out, c_out, w_out.astype(BF16), x2d,
                    ln_g.astype(F32)[None, :], ln_b.astype(F32)[None, :])
    return out.reshape(B, S, D).astype(x.dtype)
```

```python
import functools
import math

import jax
import jax.numpy as jnp
import numpy as np
from jax import lax
from jax.experimental import pallas as pl
from jax.experimental.pallas import tpu as pltpu

F32 = jnp.float32
BF16 = jnp.bfloat16

D_MODEL = 4096
MEM_LEN = 256
SWA_HEADS = 16
SWA_KV_HEADS = 4
SWA_GROUP = SWA_HEADS // SWA_KV_HEADS
SWA_HEAD_DIM = 128
WINDOW = 128
BLOCK = 128
M_HEADS = 4
M_QK_DIM = 128
M_V_DIM = 256
M_CHUNK = 128
CONV_WIDTH = 4
X_HEADS = 4
X_HEAD_DIM = 256
NUM_BUCKETS = 32
MAX_DISTANCE = 128

SWA_W = SWA_HEADS * SWA_HEAD_DIM
SWA_KV_W = SWA_KV_HEADS * SWA_HEAD_DIM
M_QK_W = M_HEADS * M_QK_DIM
M_V_W = M_HEADS * M_V_DIM
X_W = X_HEADS * X_HEAD_DIM
MIX_W = SWA_W + M_V_W + X_W

DEEPNORM_ALPHA = 2.0 ** 0.25
LN_EPS = 1e-5
HEAD_NORM_EPS = 1e-6

_REF_SPLITS = (
    ("a_q", SWA_W), ("a_k", SWA_KV_W), ("a_v", SWA_KV_W), ("a_z", SWA_W),
    ("m_q", M_QK_W), ("m_k", M_QK_W), ("m_v", M_V_W), ("m_i", M_HEADS), ("m_f", M_HEADS),
    ("m_o", M_V_W), ("m_z", M_V_W), ("c_q", X_W), ("c_z", X_W),
)
GATE_W = 128
V7X_VMEM_BYTES = 64 * 1024 * 1024
V7X_VMEM_LIMIT = V7X_VMEM_BYTES - 4 * 1024 * 1024


def _ref_offsets():
    offs, acc = {}, 0
    for name, width in _REF_SPLITS:
        offs[name] = acc
        acc += width
    return offs


_REF_OFFS = _ref_offsets()
HEAD_W = _REF_OFFS["m_i"]
GATE_COLS = 2 * M_HEADS
TAIL_START = HEAD_W + GATE_COLS
TAIL_W = 2 * M_V_W + 2 * X_W
PROJ_W = HEAD_W + TAIL_W
_HEAD_SEGS = ("a_q", "a_k", "a_v", "a_z", "m_q", "m_k", "m_v")


def _col_block(name, width):
    off = _REF_OFFS[name] - (0 if name in _HEAD_SEGS else GATE_COLS)
    assert off % width == 0, (name, off, width)
    return off // width


def _silu(z):
    return z * (1.0 / (1.0 + jnp.exp(-z)))


def _sigmoid(z):
    return 1.0 / (1.0 + jnp.exp(-z))


def _log_sigmoid(z):
    return jnp.minimum(z, 0.0) - jnp.log1p(jnp.exp(-jnp.abs(z)))


_NT = (((1,), (1,)), ((), ()))


def _in_proj_kernel(xa_ref, xb_ref, w_ref, wg_ref, o_ref, g_ref):
    half = xa_ref.shape[0]
    w = w_ref[...].astype(BF16)
    for r, x_ref in enumerate((xa_ref, xb_ref)):
        rows = slice(r * half, (r + 1) * half)
        o_ref[rows, :] = lax.dot_general(x_ref[...], w, _NT, preferred_element_type=F32).astype(o_ref.dtype)

    @pl.when(pl.program_id(1) == 0)
    def _():
        wg = wg_ref[...].astype(BF16)
        for r, x_ref in enumerate((xa_ref, xb_ref)):
            rows = slice(r * half, (r + 1) * half)
            g_ref[rows, :] = lax.dot_general(x_ref[...], wg, _NT, preferred_element_type=F32)


def _in_proj(x_bf, w_t, *, tm=2048, tn=512):
    T, K = x_bf.shape
    n_head, n_tail = HEAD_W // tn, TAIL_W // tn
    assert HEAD_W % tn == 0 and TAIL_W % tn == 0 and HEAD_W % GATE_W == 0

    def w_row(i, j):
        row = jnp.where(j < n_head, j * tn, TAIL_START + (j - n_head) * tn)
        return pl.multiple_of(row, math.gcd(tn, TAIL_START)), 0

    return pl.pallas_call(
        _in_proj_kernel,
        out_shape=(jax.ShapeDtypeStruct((T, PROJ_W), BF16), jax.ShapeDtypeStruct((T, GATE_W), F32)),
        grid=(T // tm, n_head + n_tail),
        in_specs=[
            pl.BlockSpec((tm // 2, K), lambda i, j: (2 * i, 0)),
            pl.BlockSpec((tm // 2, K), lambda i, j: (2 * i + 1, 0), pipeline_mode=pl.Buffered(1)),
            pl.BlockSpec((pl.Element(tn), pl.Element(K)), w_row),
            pl.BlockSpec((GATE_W, K), lambda i, j: (HEAD_W // GATE_W, 0)),
        ],
        out_specs=(
            pl.BlockSpec((tm, tn), lambda i, j: (i, j)),
            pl.BlockSpec((tm, GATE_W), lambda i, j: (i, 0)),
        ),
        compiler_params=pltpu.CompilerParams(
            dimension_semantics=("arbitrary", "arbitrary"),
            vmem_limit_bytes=V7X_VMEM_LIMIT),
        name="in_proj",
    )(x_bf, x_bf, w_t, w_t)


def _t5_bucket_np(dist):
    max_exact = NUM_BUCKETS // 2
    is_small = dist < max_exact
    ratio = np.maximum(dist, max_exact).astype(np.float32) / np.float32(max_exact)
    large = max_exact + (np.log(ratio) / np.float32(math.log(MAX_DISTANCE / max_exact))
                         * np.float32(NUM_BUCKETS - max_exact)).astype(np.int32)
    large = np.minimum(large, NUM_BUCKETS - 1)
    return np.where(is_small, dist, large)


def _swa_bucket_row():
    c = np.arange(2 * BLOCK, dtype=np.int32)
    return _t5_bucket_np(np.clip(BLOCK - c, 0, MAX_DISTANCE - 1)).astype(np.int32)[None, :]


def _swa_build_bias(bucket_ref, relb_ref, tbl_ref):
    q = lax.broadcasted_iota(jnp.int32, (BLOCK, 2 * BLOCK), 0)
    s = lax.broadcasted_iota(jnp.int32, (BLOCK, 2 * BLOCK), 1)
    dist = q + BLOCK - s
    in_win = jnp.where(dist >= 0, jnp.where(dist < WINDOW, 1, 0), 0)
    in_win_first = jnp.where(s >= BLOCK, in_win, 0)
    bucket = bucket_ref[...]
    for h in range(SWA_HEADS):
        u = jnp.zeros((1, 2 * BLOCK), F32)
        for bkt in range(NUM_BUCKETS):
            u = jnp.where(bucket == bkt, relb_ref[bkt * SWA_HEADS + h], u)
        t = pltpu.roll(jnp.broadcast_to(u, (BLOCK, 2 * BLOCK)), 0, 1, stride=1, stride_axis=0)
        tbl_ref[1, h] = jnp.where(in_win == 1, t, -jnp.inf)
        tbl_ref[0, h] = jnp.where(in_win_first == 1, t, -jnp.inf)


def _swa_kernel(sink_ref, relb_ref, bucket_ref, q_ref, z0_ref, z1_ref, kp_ref, kc_ref, vp_ref, vc_ref,
                o_ref, tbl_ref, *, nblk):
    scale = SWA_HEAD_DIM ** -0.5
    b, n = pl.program_id(0), pl.program_id(1)

    @pl.when((b == 0) & (n == 0))
    def _():
        _swa_build_bias(bucket_ref, relb_ref, tbl_ref)

    half = SWA_HEADS // 2
    for r in range(nblk):
        rows = slice(r * BLOCK, (r + 1) * BLOCK)
        first = jnp.minimum(n, 1) if r == 0 else 1
        for kv in range(SWA_KV_HEADS):
            ksl = slice(kv * SWA_HEAD_DIM, (kv + 1) * SWA_HEAD_DIM)
            if r == 0:
                kk = jnp.concatenate([kp_ref[:, ksl], kc_ref[rows, ksl]], axis=0)
                vv = jnp.concatenate([vp_ref[:, ksl], vc_ref[rows, ksl]], axis=0)
            else:
                kk = kc_ref[(r - 1) * BLOCK:(r + 1) * BLOCK, ksl]
                vv = vc_ref[(r - 1) * BLOCK:(r + 1) * BLOCK, ksl]
            heads = [kv * SWA_GROUP + g for g in range(SWA_GROUP)]
            q4 = jnp.concatenate(
                [q_ref[rows, h * SWA_HEAD_DIM:(h + 1) * SWA_HEAD_DIM] for h in heads], axis=0)
            s4 = lax.dot_general(q4, kk, _NT, preferred_element_type=F32)
            ps, denoms = [], []
            for g, h in enumerate(heads):
                s = s4[g * BLOCK:(g + 1) * BLOCK] * scale + tbl_ref[first, h]
                sink = sink_ref[h]
                m = jnp.maximum(jnp.max(s, axis=-1, keepdims=True), sink)
                p = jnp.exp(s - m)
                denoms.append(jnp.sum(p, axis=-1, keepdims=True) + jnp.exp(sink - m))
                ps.append(p.astype(BF16))
            o4 = jnp.dot(jnp.concatenate(ps, axis=0), vv, preferred_element_type=F32)
            for g, h in enumerate(heads):
                z_ref = z0_ref if h < half else z1_ref
                zsl = slice((h % half) * SWA_HEAD_DIM, (h % half + 1) * SWA_HEAD_DIM)
                o = o4[g * BLOCK:(g + 1) * BLOCK] / denoms[g]
                o_ref[rows, h * SWA_HEAD_DIM:(h + 1) * SWA_HEAD_DIM] = (
                    o * _silu(z_ref[rows, zsl].astype(F32))).astype(o_ref.dtype)


def _swa(proj, rel_bias, sinks, *, batch, seq, nblk=4):
    T = proj.shape[0]
    tq = nblk * BLOCK
    nt = seq // tq
    assert seq % tq == 0
    qc = _col_block("a_q", SWA_W)
    zc = _col_block("a_z", SWA_W // 2)
    kc = _col_block("a_k", SWA_KV_W)
    vc = _col_block("a_v", SWA_KV_W)

    def cur(col):
        return lambda b, n, *_: (b * nt + n, col)

    def prev(col):
        return lambda b, n, *_: (jnp.maximum((b * nt + n) * nblk - 1, 0), col)

    grid_spec = pltpu.PrefetchScalarGridSpec(
        num_scalar_prefetch=2,
        grid=(batch, nt),
        in_specs=[
            pl.BlockSpec((1, 2 * BLOCK), lambda b, n, *_: (0, 0)),
            pl.BlockSpec((tq, SWA_W), cur(qc)),
            pl.BlockSpec((tq, SWA_W // 2), cur(zc)),
            pl.BlockSpec((tq, SWA_W // 2), cur(zc + 1)),
            pl.BlockSpec((BLOCK, SWA_KV_W), prev(kc)),
            pl.BlockSpec((tq, SWA_KV_W), cur(kc)),
            pl.BlockSpec((BLOCK, SWA_KV_W), prev(vc)),
            pl.BlockSpec((tq, SWA_KV_W), cur(vc)),
        ],
        out_specs=pl.BlockSpec((tq, SWA_W), lambda b, n, *_: (b * nt + n, 0)),
        scratch_shapes=[pltpu.VMEM((2, SWA_HEADS, BLOCK, 2 * BLOCK), F32)],
    )
    return pl.pallas_call(
        functools.partial(_swa_kernel, nblk=nblk),
        out_shape=jax.ShapeDtypeStruct((T, SWA_W), BF16),
        grid_spec=grid_spec,
        compiler_params=pltpu.CompilerParams(
            dimension_semantics=("arbitrary", "arbitrary"), vmem_limit_bytes=V7X_VMEM_LIMIT),
        name="swa",
    )(sinks, rel_bias.reshape(-1), jnp.asarray(_swa_bucket_row()), proj, proj, proj, proj, proj, proj, proj)


def _cumsum_lanes(x):
    n = x.shape[-1]
    lane = lax.broadcasted_iota(jnp.int32, x.shape, x.ndim - 1)
    shift = 1
    while shift < n:
        x = x + jnp.where(lane >= shift, pltpu.roll(x, shift, x.ndim - 1), 0.0)
        shift *= 2
    return x


def _mlstm_kernel(q_ref, k_ref, v_ref, g_ref, o_ref, z_ref, cw_ref, cb_ref, gb_ref, ng_ref,
                  out_ref, xbuf, c_state, n_state, m_state):
    L = M_CHUNK
    HALO = 8

    @pl.when(pl.program_id(1) == 0)
    def _():
        xbuf[0:HALO, :] = jnp.zeros((HALO, 2 * M_QK_W), F32)
        c_state[...] = jnp.zeros_like(c_state)
        n_state[...] = jnp.zeros_like(n_state)
        m_state[...] = jnp.zeros_like(m_state)

    xbuf[HALO:HALO + L, 0:M_QK_W] = q_ref[...].astype(F32)
    xbuf[HALO:HALO + L, M_QK_W:] = k_ref[...].astype(F32)
    acc = cb_ref[...] + cw_ref[CONV_WIDTH - 1:CONV_WIDTH, :] * xbuf[HALO:HALO + L, :]
    for j in range(CONV_WIDTH - 1):
        back = CONV_WIDTH - 1 - j
        acc = acc + cw_ref[j:j + 1, :] * xbuf[HALO - back:HALO - back + L, :]
    tail = xbuf[L:L + HALO, :]
    xbuf[0:HALO, :] = tail
    qk = _silu(acc)
    q_all = qk[:, 0:M_QK_W]
    k_all = qk[:, M_QK_W:] * (M_QK_DIM ** -0.5)

    gt = (g_ref[...] + gb_ref[...]).T[0:8, :]
    row8 = lax.broadcasted_iota(jnp.int32, (8, L), 0)
    head_rows = row8 >= M_HEADS
    log_i = jnp.where(head_rows, pltpu.roll(gt, M_HEADS, 0), 0.0)
    log_f = jnp.where(head_rows, _log_sigmoid(gt), 0.0)
    b = _cumsum_lanes(log_f)
    g_tot = b[:, L - 1:L]
    a = g_tot - b + log_i
    m_loc = jnp.max(a, axis=-1, keepdims=True)
    w = jnp.exp(a - m_loc)
    m_prev = m_state[...]
    inter = b + m_prev
    r = log_i - b

    cols = jnp.concatenate([inter, w, b, jnp.zeros((L - 24, L), F32)], axis=0).T

    g_rep = jnp.broadcast_to(g_tot, (8, L))
    m_loc_rep = jnp.broadcast_to(m_loc, (8, L))
    m_new = jnp.maximum(g_rep + m_prev, m_loc_rep)
    s_prev = jnp.exp(g_rep + m_prev - m_new)
    s_loc = jnp.exp(m_loc_rep - m_new)
    n_prev_all = n_state[...]

    ti = lax.broadcasted_iota(jnp.int32, (L, L), 0)
    si = lax.broadcasted_iota(jnp.int32, (L, L), 1)
    causal = ti >= si

    for h in range(M_HEADS):
        rr = M_HEADS + h
        qh = q_all[:, h * M_QK_DIM:(h + 1) * M_QK_DIM]
        kh = k_all[:, h * M_QK_DIM:(h + 1) * M_QK_DIM]
        vh = v_ref[:, h * M_V_DIM:(h + 1) * M_V_DIM]
        inter_c = cols[:, rr:rr + 1]
        w_c = cols[:, 8 + rr:9 + rr]
        b_c = cols[:, 16 + rr:17 + rr]

        dmat = jnp.where(causal, b_c + r[rr:rr + 1, :], -jnp.inf)
        m_t = jnp.maximum(inter_c, jnp.max(dmat, axis=-1, keepdims=True))
        qb = qh.astype(BF16)
        kb = kh.astype(BF16)
        s = lax.dot_general(qb, kb, _NT, preferred_element_type=F32)
        sm = s * jnp.exp(dmat - m_t)
        w_inter = jnp.exp(inter_c - m_t)
        c_prev = c_state[h]
        num = (jnp.dot(sm.astype(BF16), vh, preferred_element_type=F32)
               + w_inter * jnp.dot(qb, c_prev.astype(BF16), preferred_element_type=F32))
        n_prev = n_prev_all[rr:rr + 1, :]
        den = (jnp.sum(sm, axis=-1, keepdims=True)
               + w_inter * jnp.sum(qh * n_prev, axis=-1, keepdims=True))
        hcell = num / jnp.maximum(jnp.abs(den), jnp.exp(-m_t))

        vsl = slice(h * M_V_DIM, (h + 1) * M_V_DIM)
        hg = _sigmoid(o_ref[:, vsl].astype(F32)) * hcell
        mu = jnp.mean(hg, axis=-1, keepdims=True)
        var = jnp.mean(jnp.square(hg - mu), axis=-1, keepdims=True)
        hn = (hg - mu) * lax.rsqrt(var + HEAD_NORM_EPS) * ng_ref[:, vsl]
        out_ref[:, vsl] = (hn * _silu(z_ref[:, vsl].astype(F32))).astype(out_ref.dtype)

        wk = w_c * kh
        c_loc = jnp.dot(wk.T.astype(BF16), vh, preferred_element_type=F32)
        sp_row = s_prev[rr:rr + 1, :]
        sl_row = s_loc[rr:rr + 1, :]
        sp2 = jnp.concatenate([sp_row, sp_row], axis=1)
        sl2 = jnp.concatenate([sl_row, sl_row], axis=1)
        c_state[h] = sp2 * c_prev + sl2 * c_loc
        n_loc = jnp.sum(wk, axis=0, keepdims=True)
        n_state[rr:rr + 1, :] = sp_row * n_prev + sl_row * n_loc

    m_state[...] = m_new


def _mlstm(proj, gates, conv_w, conv_b, gate_bias, norm_g, *, batch, seq):
    T = proj.shape[0]
    L = M_CHUNK
    nc = seq // L
    assert M_QK_DIM == L, "lane-replicated state rows assume chunk length == qk head dim"

    def tok(col):
        return lambda b, c: (b * nc + c, col)

    const = lambda b, c: (0, 0)
    return pl.pallas_call(
        _mlstm_kernel,
        out_shape=jax.ShapeDtypeStruct((T, M_V_W), BF16),
        grid=(batch, nc),
        in_specs=[
            pl.BlockSpec((L, M_QK_W), tok(_col_block("m_q", M_QK_W))),
            pl.BlockSpec((L, M_QK_W), tok(_col_block("m_k", M_QK_W))),
            pl.BlockSpec((L, M_V_W), tok(_col_block("m_v", M_V_W))),
            pl.BlockSpec((L, GATE_W), tok(0)),
            pl.BlockSpec((L, M_V_W), tok(_col_block("m_o", M_V_W))),
            pl.BlockSpec((L, M_V_W), tok(_col_block("m_z", M_V_W))),
            pl.BlockSpec((CONV_WIDTH, 2 * M_QK_W), const),
            pl.BlockSpec((1, 2 * M_QK_W), const),
            pl.BlockSpec((1, GATE_W), const),
            pl.BlockSpec((1, M_V_W), const),
        ],
        out_specs=pl.BlockSpec((L, M_V_W), tok(0)),
        scratch_shapes=[
            pltpu.VMEM((L + 8, 2 * M_QK_W), F32),
            pltpu.VMEM((M_HEADS, M_QK_DIM, M_V_DIM), F32),
            pltpu.VMEM((8, M_QK_DIM), F32),
            pltpu.VMEM((8, L), F32),
        ],
        compiler_params=pltpu.CompilerParams(dimension_semantics=("arbitrary", "arbitrary")),
        name="mlstm",
    )(proj, proj, proj, gates, proj, proj, conv_w, conv_b, gate_bias, norm_g)


def _mem_kv_kernel(mem_ref, w_ref, o_ref):
    o_ref[...] = jnp.dot(mem_ref[...].astype(BF16), w_ref[...].astype(BF16),
                         preferred_element_type=F32).astype(o_ref.dtype)


def _mem_kv(mem2d, w_mem_kv, *, tn=512):
    M, K = mem2d.shape
    N = w_mem_kv.shape[1]
    return pl.pallas_call(
        _mem_kv_kernel,
        out_shape=jax.ShapeDtypeStruct((M, N), BF16),
        grid=(N // tn,),
        in_specs=[pl.BlockSpec((M, K), lambda j: (0, 0)),
                  pl.BlockSpec((K, tn), lambda j: (0, j))],
        out_specs=pl.BlockSpec((M, tn), lambda j: (0, j)),
        compiler_params=pltpu.CompilerParams(
            dimension_semantics=("arbitrary",), vmem_limit_bytes=V7X_VMEM_LIMIT),
        name="mem_kv",
    )(mem2d, w_mem_kv)


def _xattn_kernel(q_ref, z_ref, k_ref, v_ref, o_ref):
    scale = X_HEAD_DIM ** -0.5
    for h in range(X_HEADS):
        hsl = slice(h * X_HEAD_DIM, (h + 1) * X_HEAD_DIM)
        s = lax.dot_general(q_ref[:, hsl], k_ref[:, hsl], _NT,
                            preferred_element_type=F32) * scale
        m = jnp.max(s, axis=-1, keepdims=True)
        p = jnp.exp(s - m)
        denom = jnp.sum(p, axis=-1, keepdims=True)
        o = jnp.dot(p.astype(BF16), v_ref[:, hsl], preferred_element_type=F32) / denom
        o_ref[:, hsl] = (o * _silu(z_ref[:, hsl].astype(F32))).astype(o_ref.dtype)


def _xattn(proj, mkv, *, batch, seq, tq=512):
    T = proj.shape[0]
    nq = seq // tq
    tok = lambda col: (lambda b, i: (b * nq + i, col))
    return pl.pallas_call(
        _xattn_kernel,
        out_shape=jax.ShapeDtypeStruct((T, X_W), BF16),
        grid=(batch, nq),
        in_specs=[
            pl.BlockSpec((tq, X_W), tok(_col_block("c_q", X_W))),
            pl.BlockSpec((tq, X_W), tok(_col_block("c_z", X_W))),
            pl.BlockSpec((MEM_LEN, X_W), lambda b, i: (b, 0)),
            pl.BlockSpec((MEM_LEN, X_W), lambda b, i: (b, 1)),
        ],
        out_specs=pl.BlockSpec((tq, X_W), tok(0)),
        compiler_params=pltpu.CompilerParams(dimension_semantics=("arbitrary", "arbitrary")),
        name="xattn",
    )(proj, proj, mkv, mkv)


def _out_proj_kernel(a_ref, m_ref, c_ref, wa_ref, wm_ref, wc_ref, x_ref, g_ref, b_ref, o_ref, *, tn):
    j = pl.program_id(1)
    y = (jnp.dot(a_ref[...], wa_ref[...], preferred_element_type=F32)
         + jnp.dot(m_ref[...], wm_ref[...], preferred_element_type=F32)
         + jnp.dot(c_ref[...], wc_ref[...], preferred_element_type=F32))
    col = pl.multiple_of(j * tn, tn)
    o_ref[:, pl.ds(col, tn)] = DEEPNORM_ALPHA * x_ref[...] + y

    @pl.when(j == pl.num_programs(1) - 1)
    def _():
        r = o_ref[...]
        mu = jnp.mean(r, axis=-1, keepdims=True)
        d = r - mu
        var = jnp.mean(jnp.square(d), axis=-1, keepdims=True)
        o_ref[...] = d * lax.rsqrt(var + LN_EPS) * g_ref[...] + b_ref[...]


def _out_proj(a_out, m_out, c_out, w_out_bf, x2d, ln_g, ln_b, *, tm=512, tn=1024):
    T, N = x2d.shape
    return pl.pallas_call(
        functools.partial(_out_proj_kernel, tn=tn),
        out_shape=jax.ShapeDtypeStruct((T, N), F32),
        grid=(T // tm, N // tn),
        in_specs=[
            pl.BlockSpec((tm, SWA_W), lambda i, j: (i, 0)),
            pl.BlockSpec((tm, M_V_W), lambda i, j: (i, 0)),
            pl.BlockSpec((tm, X_W), lambda i, j: (i, 0)),
            pl.BlockSpec((SWA_W, tn), lambda i, j: (0, j)),
            pl.BlockSpec((M_V_W, tn), lambda i, j: (SWA_W // M_V_W, j)),
            pl.BlockSpec((X_W, tn), lambda i, j: ((SWA_W + M_V_W) // X_W, j)),
            pl.BlockSpec((tm, tn), lambda i, j: (i, j)),
            pl.BlockSpec((1, N), lambda i, j: (0, 0)),
            pl.BlockSpec((1, N), lambda i, j: (0, 0)),
        ],
        out_specs=pl.BlockSpec((tm, N), lambda i, j: (i, 0)),
        compiler_params=pltpu.CompilerParams(
            dimension_semantics=("arbitrary", "arbitrary"), vmem_limit_bytes=V7X_VMEM_LIMIT),
        name="out_proj",
    )(a_out, m_out, c_out, w_out_bf, w_out_bf, w_out_bf, x2d, ln_g, ln_b)


def kernel(x, mem, w_in, conv_w, conv_b, b_i, b_f, m_norm_g, rel_bias, sinks, w_mem_kv, w_out, ln_g, ln_b):
    B, S, D = x.shape
    assert D == D_MODEL and S % BLOCK == 0 and S % M_CHUNK == 0
    T = B * S
    x2d = x.reshape(T, D)
    gate_bias = jnp.concatenate(
        [b_i.astype(F32), b_f.astype(F32), jnp.zeros((GATE_W - GATE_COLS,), F32)])[None, :]

    proj, gates = _in_proj(x2d.astype(BF16), w_in.T)

    a_out = _swa(proj, rel_bias.astype(F32), sinks.astype(F32), batch=B, seq=S)
    m_out = _mlstm(proj, gates, conv_w.astype(F32), conv_b.astype(F32)[None, :], gate_bias,
                   m_norm_g.astype(F32)[None, :], batch=B, seq=S)
    mkv = _mem_kv(mem.reshape(B * MEM_LEN, D), w_mem_kv)
    c_out = _xattn(proj, mkv, batch=B, seq=S)

    out = _out_proj(a_out, m_out, c_out, w_out.astype(BF16), x2d,
                    ln_g.astype(F32)[None, :], ln_b.astype(F32)[None, :])
    return out.reshape(B, S, D).astype(x.dtype)
```

```python
import functools
import math

import jax
import jax.numpy as jnp
import numpy as np
from jax import lax
from jax.experimental import pallas as pl
from jax.experimental.pallas import tpu as pltpu

F32 = jnp.float32
BF16 = jnp.bfloat16

D_MODEL = 4096
MEM_LEN = 256
SWA_HEADS = 16
SWA_KV_HEADS = 4
SWA_GROUP = SWA_HEADS // SWA_KV_HEADS
SWA_HEAD_DIM = 128
WINDOW = 128
BLOCK = 128
M_HEADS = 4
M_QK_DIM = 128
M_V_DIM = 256
M_CHUNK = 128
CONV_WIDTH = 4
X_HEADS = 4
X_HEAD_DIM = 256
NUM_BUCKETS = 32
MAX_DISTANCE = 128

SWA_W = SWA_HEADS * SWA_HEAD_DIM
SWA_KV_W = SWA_KV_HEADS * SWA_HEAD_DIM
M_QK_W = M_HEADS * M_QK_DIM
M_V_W = M_HEADS * M_V_DIM
X_W = X_HEADS * X_HEAD_DIM
MIX_W = SWA_W + M_V_W + X_W

DEEPNORM_ALPHA = 2.0 ** 0.25
LN_EPS = 1e-5
HEAD_NORM_EPS = 1e-6

_REF_SPLITS = (
    ("a_q", SWA_W), ("a_k", SWA_KV_W), ("a_v", SWA_KV_W), ("a_z", SWA_W),
    ("m_q", M_QK_W), ("m_k", M_QK_W), ("m_v", M_V_W), ("m_i", M_HEADS), ("m_f", M_HEADS),
    ("m_o", M_V_W), ("m_z", M_V_W), ("c_q", X_W), ("c_z", X_W),
)
GATE_W = 128
V7X_VMEM_BYTES = 64 * 1024 * 1024
V7X_VMEM_LIMIT = V7X_VMEM_BYTES - 4 * 1024 * 1024


def _ref_offsets():
    offs, acc = {}, 0
    for name, width in _REF_SPLITS:
        offs[name] = acc
        acc += width
    return offs


_REF_OFFS = _ref_offsets()
HEAD_W = _REF_OFFS["m_i"]
GATE_COLS = 2 * M_HEADS
TAIL_START = HEAD_W + GATE_COLS
TAIL_W = 2 * M_V_W + 2 * X_W
PROJ_W = HEAD_W + TAIL_W
_HEAD_SEGS = ("a_q", "a_k", "a_v", "a_z", "m_q", "m_k", "m_v")


def _col_block(name, width):
    off = _REF_OFFS[name] - (0 if name in _HEAD_SEGS else GATE_COLS)
    assert off % width == 0, (name, off, width)
    return off // width


def _silu(z):
    return z * (1.0 / (1.0 + jnp.exp(-z)))


def _sigmoid(z):
    return 1.0 / (1.0 + jnp.exp(-z))


def _log_sigmoid(z):
    return jnp.minimum(z, 0.0) - jnp.log1p(jnp.exp(-jnp.abs(z)))


_NT = (((1,), (1,)), ((), ()))


def _in_proj_kernel(xa_ref, xb_ref, w_ref, wg_ref, o_ref, g_ref):
    half = xa_ref.shape[0]
    w = w_ref[...].astype(BF16)
    for r, x_ref in enumerate((xa_ref, xb_ref)):
        rows = slice(r * half, (r + 1) * half)
        o_ref[rows, :] = lax.dot_general(x_ref[...], w, _NT, preferred_element_type=F32).astype(o_ref.dtype)

    @pl.when(pl.program_id(1) == 0)
    def _():
        wg = wg_ref[...].astype(BF16)
        for r, x_ref in enumerate((xa_ref, xb_ref)):
            rows = slice(r * half, (r + 1) * half)
            g_ref[rows, :] = lax.dot_general(x_ref[...], wg, _NT, preferred_element_type=F32)


def _in_proj(x_bf, w_t, *, tm=2048, tn=512):
    T, K = x_bf.shape
    n_head, n_tail = HEAD_W // tn, TAIL_W // tn
    assert HEAD_W % tn == 0 and TAIL_W % tn == 0 and HEAD_W % GATE_W == 0

    def w_row(i, j):
        row = jnp.where(j < n_head, j * tn, TAIL_START + (j - n_head) * tn)
        return pl.multiple_of(row, math.gcd(tn, TAIL_START)), 0

    return pl.pallas_call(
        _in_proj_kernel,
        out_shape=(jax.ShapeDtypeStruct((T, PROJ_W), BF16), jax.ShapeDtypeStruct((T, GATE_W), F32)),
        grid=(T // tm, n_head + n_tail),
        in_specs=[
            pl.BlockSpec((tm // 2, K), lambda i, j: (2 * i, 0)),
            pl.BlockSpec((tm // 2, K), lambda i, j: (2 * i + 1, 0), pipeline_mode=pl.Buffered(1)),
            pl.BlockSpec((pl.Element(tn), pl.Element(K)), w_row),
            pl.BlockSpec((GATE_W, K), lambda i, j: (HEAD_W // GATE_W, 0)),
        ],
        out_specs=(
            pl.BlockSpec((tm, tn), lambda i, j: (i, j)),
            pl.BlockSpec((tm, GATE_W), lambda i, j: (i, 0)),
        ),
        compiler_params=pltpu.CompilerParams(
            dimension_semantics=("arbitrary", "arbitrary"),
            vmem_limit_bytes=V7X_VMEM_LIMIT),
        name="in_proj",
    )(x_bf, x_bf, w_t, w_t)


def _t5_bucket_np(dist):
    max_exact = NUM_BUCKETS // 2
    is_small = dist < max_exact
    ratio = np.maximum(dist, max_exact).astype(np.float32) / np.float32(max_exact)
    large = max_exact + (np.log(ratio) / np.float32(math.log(MAX_DISTANCE / max_exact))
                         * np.float32(NUM_BUCKETS - max_exact)).astype(np.int32)
    large = np.minimum(large, NUM_BUCKETS - 1)
    return np.where(is_small, dist, large)


def _swa_bucket_row():
    c = np.arange(2 * BLOCK, dtype=np.int32)
    return _t5_bucket_np(np.clip(BLOCK - c, 0, MAX_DISTANCE - 1)).astype(np.int32)[None, :]


def _swa_build_bias(bucket_ref, relb_ref, tbl_ref):
    q = lax.broadcasted_iota(jnp.int32, (BLOCK, 2 * BLOCK), 0)
    s = lax.broadcasted_iota(jnp.int32, (BLOCK, 2 * BLOCK), 1)
    dist = q + BLOCK - s
    in_win = jnp.where(dist >= 0, jnp.where(dist < WINDOW, 1, 0), 0)
    bucket = bucket_ref[...]
    for h in range(SWA_HEADS):
        u = jnp.zeros((1, 2 * BLOCK), F32)
        for bkt in range(NUM_BUCKETS):
            u = jnp.where(bucket == bkt, relb_ref[bkt * SWA_HEADS + h], u)
        t = pltpu.roll(jnp.broadcast_to(u, (BLOCK, 2 * BLOCK)), 0, 1, stride=1, stride_axis=0)
        tbl_ref[h] = jnp.where(in_win == 1, t, -jnp.inf)


def _swa_block(sink_ref, q_ref, z0_ref, z1_ref, kp_ref, kc_ref, vp_ref, vc_ref, tbl_ref, first_mask, out,
               kv_heads):
    scale = SWA_HEAD_DIM ** -0.5
    half = SWA_HEADS // 2
    for kv in kv_heads:
        ksl = slice(kv * SWA_HEAD_DIM, (kv + 1) * SWA_HEAD_DIM)
        kk = jnp.concatenate([kp_ref[:, ksl], kc_ref[:, ksl]], axis=0)
        vv = jnp.concatenate([vp_ref[:, ksl], vc_ref[:, ksl]], axis=0)
        heads = [kv * SWA_GROUP + g for g in range(SWA_GROUP)]
        q4 = jnp.concatenate(
            [q_ref[:, h * SWA_HEAD_DIM:(h + 1) * SWA_HEAD_DIM] for h in heads], axis=0)
        s4 = lax.dot_general(q4, kk, _NT, preferred_element_type=F32)
        ps, denoms = [], []
        for g, h in enumerate(heads):
            s = s4[g * BLOCK:(g + 1) * BLOCK] * scale + (tbl_ref[h] + first_mask)
            sink = sink_ref[h]
            m = jnp.maximum(jnp.max(s, axis=-1, keepdims=True), sink)
            p = jnp.exp(s - m)
            denoms.append(jnp.sum(p, axis=-1, keepdims=True) + jnp.exp(sink - m))
            ps.append(p.astype(BF16))
        o4 = jnp.dot(jnp.concatenate(ps, axis=0), vv, preferred_element_type=F32)
        for g, h in enumerate(heads):
            z_ref = z0_ref if h < half else z1_ref
            zsl = slice((h % half) * SWA_HEAD_DIM, (h % half + 1) * SWA_HEAD_DIM)
            o = o4[g * BLOCK:(g + 1) * BLOCK] / denoms[g]
            out[:, h * SWA_HEAD_DIM:(h + 1) * SWA_HEAD_DIM] = (
                o * _silu(z_ref[:, zsl].astype(F32))).astype(out.dtype)


def _cumsum_lanes(x):
    n = x.shape[-1]
    lane = lax.broadcasted_iota(jnp.int32, x.shape, x.ndim - 1)
    shift = 1
    while shift < n:
        x = x + jnp.where(lane >= shift, pltpu.roll(x, shift, x.ndim - 1), 0.0)
        shift *= 2
    return x


_CONV_HALO = 8


def _mlstm_reset(xbuf, c_state, n_state, m_state):
    xbuf[0:_CONV_HALO, :] = jnp.zeros((_CONV_HALO, 2 * M_QK_W), F32)
    c_state[...] = jnp.zeros_like(c_state)
    n_state[...] = jnp.zeros_like(n_state)
    m_state[...] = jnp.zeros_like(m_state)


def _mlstm_chunk(q_ref, k_ref, v_ref, g_ref, o_ref, z_ref, cw_ref, cb_ref, gb_ref, ng_ref,
                 xbuf, c_state, n_state, m_state, out, col0):
    L = M_CHUNK
    HALO = _CONV_HALO

    xbuf[HALO:HALO + L, 0:M_QK_W] = q_ref[...].astype(F32)
    xbuf[HALO:HALO + L, M_QK_W:] = k_ref[...].astype(F32)
    acc = cb_ref[...] + cw_ref[CONV_WIDTH - 1:CONV_WIDTH, :] * xbuf[HALO:HALO + L, :]
    for j in range(CONV_WIDTH - 1):
        back = CONV_WIDTH - 1 - j
        acc = acc + cw_ref[j:j + 1, :] * xbuf[HALO - back:HALO - back + L, :]
    tail = xbuf[L:L + HALO, :]
    xbuf[0:HALO, :] = tail
    qk = _silu(acc)
    q_all = qk[:, 0:M_QK_W]
    k_all = qk[:, M_QK_W:] * (M_QK_DIM ** -0.5)

    gt = (g_ref[...] + gb_ref[...]).T[0:8, :]
    row8 = lax.broadcasted_iota(jnp.int32, (8, L), 0)
    head_rows = row8 >= M_HEADS
    log_i = jnp.where(head_rows, pltpu.roll(gt, M_HEADS, 0), 0.0)
    log_f = jnp.where(head_rows, _log_sigmoid(gt), 0.0)
    b = _cumsum_lanes(log_f)
    g_tot = b[:, L - 1:L]
    a = g_tot - b + log_i
    m_loc = jnp.max(a, axis=-1, keepdims=True)
    w = jnp.exp(a - m_loc)
    m_prev = m_state[...]
    inter = b + m_prev
    r = log_i - b

    cols = jnp.concatenate([inter, w, b, jnp.zeros((L - 24, L), F32)], axis=0).T

    g_rep = jnp.broadcast_to(g_tot, (8, L))
    m_loc_rep = jnp.broadcast_to(m_loc, (8, L))
    m_new = jnp.maximum(g_rep + m_prev, m_loc_rep)
    s_prev = jnp.exp(g_rep + m_prev - m_new)
    s_loc = jnp.exp(m_loc_rep - m_new)
    n_prev_all = n_state[...]

    ti = lax.broadcasted_iota(jnp.int32, (L, L), 0)
    si = lax.broadcasted_iota(jnp.int32, (L, L), 1)
    causal = ti >= si

    for h in range(M_HEADS):
        rr = M_HEADS + h
        qh = q_all[:, h * M_QK_DIM:(h + 1) * M_QK_DIM]
        kh = k_all[:, h * M_QK_DIM:(h + 1) * M_QK_DIM]
        vsl = slice(h * M_V_DIM, (h + 1) * M_V_DIM)
        vh = v_ref[:, vsl]
        inter_c = cols[:, rr:rr + 1]
        w_c = cols[:, 8 + rr:9 + rr]
        b_c = cols[:, 16 + rr:17 + rr]

        dmat = jnp.where(causal, b_c + r[rr:rr + 1, :], -jnp.inf)
        m_t = jnp.maximum(inter_c, jnp.max(dmat, axis=-1, keepdims=True))
        qb = qh.astype(BF16)
        kb = kh.astype(BF16)
        s = lax.dot_general(qb, kb, _NT, preferred_element_type=F32)
        sm = s * jnp.exp(dmat - m_t)
        w_inter = jnp.exp(inter_c - m_t)
        c_prev = c_state[h]
        num = (jnp.dot(sm.astype(BF16), vh, preferred_element_type=F32)
               + w_inter * jnp.dot(qb, c_prev.astype(BF16), preferred_element_type=F32))
        n_prev = n_prev_all[rr:rr + 1, :]
        den = (jnp.sum(sm, axis=-1, keepdims=True)
               + w_inter * jnp.sum(qh * n_prev, axis=-1, keepdims=True))
        hcell = num / jnp.maximum(jnp.abs(den), jnp.exp(-m_t))

        hg = _sigmoid(o_ref[:, vsl].astype(F32)) * hcell
        mu = jnp.mean(hg, axis=-1, keepdims=True)
        var = jnp.mean(jnp.square(hg - mu), axis=-1, keepdims=True)
        hn = (hg - mu) * lax.rsqrt(var + HEAD_NORM_EPS) * ng_ref[:, vsl]
        out[:, col0 + h * M_V_DIM:col0 + (h + 1) * M_V_DIM] = (
            hn * _silu(z_ref[:, vsl].astype(F32))).astype(out.dtype)

        wk = w_c * kh
        c_loc = jnp.dot(wk.T.astype(BF16), vh, preferred_element_type=F32)
        sp_row = s_prev[rr:rr + 1, :]
        sl_row = s_loc[rr:rr + 1, :]
        sp2 = jnp.concatenate([sp_row, sp_row], axis=1)
        sl2 = jnp.concatenate([sl_row, sl_row], axis=1)
        c_state[h] = sp2 * c_prev + sl2 * c_loc
        n_loc = jnp.sum(wk, axis=0, keepdims=True)
        n_state[rr:rr + 1, :] = sp_row * n_prev + sl_row * n_loc

    m_state[...] = m_new


def _mem_kv_kernel(mem_ref, w_ref, o_ref):
    o_ref[...] = jnp.dot(mem_ref[...].astype(BF16), w_ref[...].astype(BF16),
                         preferred_element_type=F32).astype(o_ref.dtype)


def _mem_kv(mem2d, w_mem_kv, *, tn=512):
    M, K = mem2d.shape
    N = w_mem_kv.shape[1]
    return pl.pallas_call(
        _mem_kv_kernel,
        out_shape=jax.ShapeDtypeStruct((M, N), BF16),
        grid=(N // tn,),
        in_specs=[pl.BlockSpec((M, K), lambda j: (0, 0)),
                  pl.BlockSpec((K, tn), lambda j: (0, j))],
        out_specs=pl.BlockSpec((M, tn), lambda j: (0, j)),
        compiler_params=pltpu.CompilerParams(
            dimension_semantics=("arbitrary",), vmem_limit_bytes=V7X_VMEM_LIMIT),
        name="mem_kv",
    )(mem2d, w_mem_kv)


def _xattn_block(q_ref, z_ref, k_ref, v_ref, out, col0):
    scale = X_HEAD_DIM ** -0.5
    for h in range(X_HEADS):
        hsl = slice(h * X_HEAD_DIM, (h + 1) * X_HEAD_DIM)
        s = lax.dot_general(q_ref[:, hsl], k_ref[:, hsl], _NT, preferred_element_type=F32) * scale
        m = jnp.max(s, axis=-1, keepdims=True)
        p = jnp.exp(s - m)
        denom = jnp.sum(p, axis=-1, keepdims=True)
        o = jnp.dot(p.astype(BF16), v_ref[:, hsl], preferred_element_type=F32) / denom
        out[:, col0 + h * X_HEAD_DIM:col0 + (h + 1) * X_HEAD_DIM] = (
            o * _silu(z_ref[:, hsl].astype(F32))).astype(out.dtype)


_LANES = 128


def _fused_kernel(sink_ref, relb_ref, bucket_ref,
                  q_ref, z0_ref, z1_ref, kp_ref, kc_ref, vp_ref, vc_ref,
                  mq_ref, mk_ref, mv_ref, g_ref, mo_ref, mz_ref, cw_ref, cb_ref, gb_ref, ng_ref,
                  cq_ref, cz_ref, memk_ref, memv_ref,
                  w_ref, x_ref, xc_ref, lng_ref, lnb_ref,
                  o_ref,
                  mix_cur, mix_new, rbuf, s1, s2, mean_s, rstd_s, tbl, xbuf, c_state, n_state, m_state,
                  *, n_tiles, blocks_per_seq):
    i, j = pl.program_id(0), pl.program_id(1)
    nj = pl.num_programs(1)
    tm, tn = x_ref.shape
    d_model = rbuf.shape[1]
    blk = jnp.minimum(i, n_tiles - 1) * nj + j
    col = pl.multiple_of(j * tn, tn)

    @pl.when((i == 0) & (j == 0))
    def _():
        _swa_build_bias(bucket_ref, relb_ref, tbl)
        mix_cur[...] = jnp.zeros_like(mix_cur)
        mix_new[...] = jnp.zeros_like(mix_new)
        rbuf[...] = jnp.zeros_like(rbuf)
        s1[...] = jnp.zeros_like(s1)
        s2[...] = jnp.zeros_like(s2)
        mean_s[...] = jnp.zeros_like(mean_s)
        rstd_s[...] = jnp.zeros_like(rstd_s)

    @pl.when((i < n_tiles) & (blk % blocks_per_seq == 0))
    def _():
        _mlstm_reset(xbuf, c_state, n_state, m_state)

    def shift():
        return jnp.broadcast_to(DEEPNORM_ALPHA * xc_ref[:, 0:1], (tm, _LANES))

    def stage_c():
        mean, rstd = mean_s[...], rstd_s[...]
        for c in range(tn // _LANES):
            sl = slice(c * _LANES, (c + 1) * _LANES)
            rq = rbuf[:, pl.ds(col + c * _LANES, _LANES)]
            o_ref[:, sl] = (rq - mean) * rstd * lng_ref[:, sl] + lnb_ref[:, sl]

    def stage_b(c, n_chunks):
        rh = tm // n_chunks
        rsl = slice(c * rh, (c + 1) * rh)
        y = jnp.dot(mix_cur[rsl, :], w_ref[...], preferred_element_type=F32)
        r = DEEPNORM_ALPHA * x_ref[rsl, :] + y
        rbuf[rsl, pl.ds(col, tn)] = r
        c0 = DEEPNORM_ALPHA * jnp.broadcast_to(xc_ref[rsl, 0:1], (rh, _LANES))
        a1, a2 = s1[rsl, :], s2[rsl, :]
        for l in range(tn // _LANES):
            d = r[:, l * _LANES:(l + 1) * _LANES] - c0
            a1 = a1 + d
            a2 = a2 + d * d
        s1[rsl, :] = a1
        s2[rsl, :] = a2

    def stage_a_pieces():
        out = mix_new.at[pl.ds(pl.multiple_of(j * BLOCK, BLOCK), BLOCK)]
        lane = lax.broadcasted_iota(jnp.int32, (1, 2 * BLOCK), 1)
        neg = jnp.where(blk % blocks_per_seq == 0, -jnp.inf, 0.0)
        first_mask = jnp.where(lane < BLOCK, neg, 0.0)
        swa = functools.partial(_swa_block, sink_ref, q_ref, z0_ref, z1_ref, kp_ref, kc_ref, vp_ref, vc_ref,
                                tbl, first_mask, out)
        half = SWA_KV_HEADS // 2
        return [
            lambda: swa(range(0, half)),
            lambda: swa(range(half, SWA_KV_HEADS)),
            lambda: _mlstm_chunk(mq_ref, mk_ref, mv_ref, g_ref, mo_ref, mz_ref, cw_ref, cb_ref, gb_ref, ng_ref,
                                 xbuf, c_state, n_state, m_state, out, SWA_W),
            lambda: _xattn_block(cq_ref, cz_ref, memk_ref, memv_ref, out, SWA_W + M_V_W),
        ]

    @pl.when(i < n_tiles)
    def _():
        stage_c()
        pieces = stage_a_pieces()
        for c, piece in enumerate(pieces):
            piece()
            stage_b(c, len(pieces))

    @pl.when(i == n_tiles)
    def _():
        stage_c()
        stage_b(0, 1)

    @pl.when(i == n_tiles + 1)
    def _():
        stage_c()

    @pl.when(j == nj - 1)
    def _():
        mean_d = jnp.sum(s1[...], axis=-1, keepdims=True) * (1.0 / d_model)
        var = jnp.sum(s2[...], axis=-1, keepdims=True) * (1.0 / d_model) - mean_d * mean_d
        mean_s[...] = shift() + jnp.broadcast_to(mean_d, (tm, _LANES))
        rstd_s[...] = jnp.broadcast_to(lax.rsqrt(var + LN_EPS), (tm, _LANES))
        s1[...] = jnp.zeros_like(s1)
        s2[...] = jnp.zeros_like(s2)
        mix_cur[...] = mix_new[...]


def _fused_mix_out(proj, gates, mkv, w_out_bf, x2d, rel_bias, sinks, conv_w, conv_b, gate_bias, norm_g,
                   ln_g, ln_b, *, batch, seq, tm=512, tn=1024):
    T, D = x2d.shape
    n_tiles = T // tm
    nj = D // tn
    blocks_per_seq = seq // BLOCK
    assert tm // BLOCK == nj and T % tm == 0 and D % tn == 0 and M_CHUNK == BLOCK and MIX_W == D

    def blk(i, j):
        return jnp.minimum(i, n_tiles - 1) * nj + j

    def cur(col):
        return lambda i, j, *_: (blk(i, j), col)

    def prev(col):
        return lambda i, j, *_: (jnp.maximum(blk(i, j) - 1, 0), col)

    def mem(col):
        return lambda i, j, *_: (blk(i, j) // blocks_per_seq, col)

    def const(i, j, *_):
        return (0, 0)

    def row_b(i):
        return jnp.clip(i - 1, 0, n_tiles - 1)

    qc = _col_block("a_q", SWA_W)
    zc = _col_block("a_z", SWA_W // 2)
    kc = _col_block("a_k", SWA_KV_W)
    vc = _col_block("a_v", SWA_KV_W)
    in_specs = [
        pl.BlockSpec((1, 2 * BLOCK), const),
        pl.BlockSpec((BLOCK, SWA_W), cur(qc)),
        pl.BlockSpec((BLOCK, SWA_W // 2), cur(zc)),
        pl.BlockSpec((BLOCK, SWA_W // 2), cur(zc + 1)),
        pl.BlockSpec((BLOCK, SWA_KV_W), prev(kc)),
        pl.BlockSpec((BLOCK, SWA_KV_W), cur(kc)),
        pl.BlockSpec((BLOCK, SWA_KV_W), prev(vc)),
        pl.BlockSpec((BLOCK, SWA_KV_W), cur(vc)),
        pl.BlockSpec((BLOCK, M_QK_W), cur(_col_block("m_q", M_QK_W))),
        pl.BlockSpec((BLOCK, M_QK_W), cur(_col_block("m_k", M_QK_W))),
        pl.BlockSpec((BLOCK, M_V_W), cur(_col_block("m_v", M_V_W))),
        pl.BlockSpec((BLOCK, GATE_W), cur(0)),
        pl.BlockSpec((BLOCK, M_V_W), cur(_col_block("m_o", M_V_W))),
        pl.BlockSpec((BLOCK, M_V_W), cur(_col_block("m_z", M_V_W))),
        pl.BlockSpec((CONV_WIDTH, 2 * M_QK_W), const),
        pl.BlockSpec((1, 2 * M_QK_W), const),
        pl.BlockSpec((1, GATE_W), const),
        pl.BlockSpec((1, M_V_W), const),
        pl.BlockSpec((BLOCK, X_W), cur(_col_block("c_q", X_W))),
        pl.BlockSpec((BLOCK, X_W), cur(_col_block("c_z", X_W))),
        pl.BlockSpec((MEM_LEN, X_W), mem(0), pipeline_mode=pl.Buffered(1)),
        pl.BlockSpec((MEM_LEN, X_W), mem(1), pipeline_mode=pl.Buffered(1)),
        pl.BlockSpec((D, tn), lambda i, j, *_: (0, j)),
        pl.BlockSpec((tm, tn), lambda i, j, *_: (row_b(i), j)),
        pl.BlockSpec((tm, _LANES), lambda i, j, *_: (row_b(i), 0)),
        pl.BlockSpec((1, tn), lambda i, j, *_: (0, j)),
        pl.BlockSpec((1, tn), lambda i, j, *_: (0, j)),
    ]
    grid_spec = pltpu.PrefetchScalarGridSpec(
        num_scalar_prefetch=2,
        grid=(n_tiles + 2, nj),
        in_specs=in_specs,
        out_specs=pl.BlockSpec((tm, tn), lambda i, j, *_: (jnp.maximum(i - 2, 0), jnp.where(i < 2, 0, j))),
        scratch_shapes=[
            pltpu.VMEM((tm, D), BF16),
            pltpu.VMEM((tm, D), BF16),
            pltpu.VMEM((tm, D), F32),
            pltpu.VMEM((tm, _LANES), F32),
            pltpu.VMEM((tm, _LANES), F32),
            pltpu.VMEM((tm, _LANES), F32),
            pltpu.VMEM((tm, _LANES), F32),
            pltpu.VMEM((SWA_HEADS, BLOCK, 2 * BLOCK), F32),
            pltpu.VMEM((M_CHUNK + _CONV_HALO, 2 * M_QK_W), F32),
            pltpu.VMEM((M_HEADS, M_QK_DIM, M_V_DIM), F32),
            pltpu.VMEM((8, M_QK_DIM), F32),
            pltpu.VMEM((8, M_CHUNK), F32),
        ],
    )
    return pl.pallas_call(
        functools.partial(_fused_kernel, n_tiles=n_tiles, blocks_per_seq=blocks_per_seq),
        out_shape=jax.ShapeDtypeStruct((T, D), F32),
        grid_spec=grid_spec,
        compiler_params=pltpu.CompilerParams(
            dimension_semantics=("arbitrary", "arbitrary"), vmem_limit_bytes=V7X_VMEM_LIMIT),
        name="mix_out",
    )(sinks, rel_bias.reshape(-1), jnp.asarray(_swa_bucket_row()),
      proj, proj, proj, proj, proj, proj, proj,
      proj, proj, proj, gates, proj, proj, conv_w, conv_b, gate_bias, norm_g,
      proj, proj, mkv, mkv,
      w_out_bf, x2d, x2d, ln_g, ln_b)


def kernel(x, mem, w_in, conv_w, conv_b, b_i, b_f, m_norm_g, rel_bias, sinks, w_mem_kv, w_out, ln_g, ln_b):
    B, S, D = x.shape
    assert D == D_MODEL and S % BLOCK == 0 and S % M_CHUNK == 0
    T = B * S
    x2d = x.reshape(T, D)
    gate_bias = jnp.concatenate(
        [b_i.astype(F32), b_f.astype(F32), jnp.zeros((GATE_W - GATE_COLS,), F32)])[None, :]

    proj, gates = _in_proj(x2d.astype(BF16), w_in.T)
    mkv = _mem_kv(mem.reshape(B * MEM_LEN, D), w_mem_kv)
    out = _fused_mix_out(proj, gates, mkv, w_out.astype(BF16), x2d, rel_bias.astype(F32), sinks.astype(F32),
                         conv_w.astype(F32), conv_b.astype(F32)[None, :], gate_bias,
                         m_norm_g.astype(F32)[None, :], ln_g.astype(F32)[None, :], ln_b.astype(F32)[None, :],
                         batch=B, seq=S)
    return out.reshape(B, S, D).astype(x.dtype)
```

```python
import functools
import math

import jax
import jax.numpy as jnp
import numpy as np
from jax import lax
from jax.experimental import pallas as pl
from jax.experimental.pallas import tpu as pltpu

F32 = jnp.float32
BF16 = jnp.bfloat16

D_MODEL = 4096
MEM_LEN = 256
SWA_HEADS = 16
SWA_KV_HEADS = 4
SWA_GROUP = SWA_HEADS // SWA_KV_HEADS
SWA_HEAD_DIM = 128
WINDOW = 128
BLOCK = 128
M_HEADS = 4
M_QK_DIM = 128
M_V_DIM = 256
M_CHUNK = 128
CONV_WIDTH = 4
X_HEADS = 4
X_HEAD_DIM = 256
NUM_BUCKETS = 32
MAX_DISTANCE = 128

SWA_W = SWA_HEADS * SWA_HEAD_DIM
SWA_KV_W = SWA_KV_HEADS * SWA_HEAD_DIM
M_QK_W = M_HEADS * M_QK_DIM
M_V_W = M_HEADS * M_V_DIM
X_W = X_HEADS * X_HEAD_DIM
MIX_W = SWA_W + M_V_W + X_W

DEEPNORM_ALPHA = 2.0 ** 0.25
LN_EPS = 1e-5
HEAD_NORM_EPS = 1e-6

_REF_SPLITS = (
    ("a_q", SWA_W), ("a_k", SWA_KV_W), ("a_v", SWA_KV_W), ("a_z", SWA_W),
    ("m_q", M_QK_W), ("m_k", M_QK_W), ("m_v", M_V_W), ("m_i", M_HEADS), ("m_f", M_HEADS),
    ("m_o", M_V_W), ("m_z", M_V_W), ("c_q", X_W), ("c_z", X_W),
)
GATE_W = 128
V7X_VMEM_BYTES = 64 * 1024 * 1024
V7X_VMEM_LIMIT = V7X_VMEM_BYTES - 4 * 1024 * 1024


def _ref_offsets():
    offs, acc = {}, 0
    for name, width in _REF_SPLITS:
        offs[name] = acc
        acc += width
    return offs


_REF_OFFS = _ref_offsets()
HEAD_W = _REF_OFFS["m_i"]
GATE_COLS = 2 * M_HEADS
TAIL_START = HEAD_W + GATE_COLS
TAIL_W = 2 * M_V_W + 2 * X_W
PROJ_W = HEAD_W + TAIL_W
_HEAD_SEGS = ("a_q", "a_k", "a_v", "a_z", "m_q", "m_k", "m_v")


def _proj_cols(name):
    off = _REF_OFFS[name] - (0 if name in _HEAD_SEGS else GATE_COLS)
    return slice(off, off + dict(_REF_SPLITS)[name])


def _silu(z):
    return z * (1.0 / (1.0 + jnp.exp(-z)))


def _sigmoid(z):
    return 1.0 / (1.0 + jnp.exp(-z))


def _log_sigmoid(z):
    return jnp.minimum(z, 0.0) - jnp.log1p(jnp.exp(-jnp.abs(z)))


_NT = (((1,), (1,)), ((), ()))


def _in_proj_kernel(xa_ref, xb_ref, w_ref, wg_ref, o_ref, g_ref):
    half = xa_ref.shape[0]
    w = w_ref[...].astype(BF16)
    for r, x_ref in enumerate((xa_ref, xb_ref)):
        rows = slice(r * half, (r + 1) * half)
        o_ref[rows, :] = lax.dot_general(x_ref[...], w, _NT, preferred_element_type=F32).astype(o_ref.dtype)

    @pl.when(pl.program_id(1) == 0)
    def _():
        wg = wg_ref[...].astype(BF16)
        for r, x_ref in enumerate((xa_ref, xb_ref)):
            rows = slice(r * half, (r + 1) * half)
            g_ref[rows, :] = lax.dot_general(x_ref[...], wg, _NT, preferred_element_type=F32)


def _in_proj(x_bf, w_t, *, tm=2048, tn=512):
    T, K = x_bf.shape
    n_head, n_tail = HEAD_W // tn, TAIL_W // tn
    assert HEAD_W % tn == 0 and TAIL_W % tn == 0 and HEAD_W % GATE_W == 0

    def w_row(i, j):
        row = jnp.where(j < n_head, j * tn, TAIL_START + (j - n_head) * tn)
        return pl.multiple_of(row, math.gcd(tn, TAIL_START)), 0

    return pl.pallas_call(
        _in_proj_kernel,
        out_shape=(jax.ShapeDtypeStruct((T, PROJ_W), BF16), jax.ShapeDtypeStruct((T, GATE_W), F32)),
        grid=(T // tm, n_head + n_tail),
        in_specs=[
            pl.BlockSpec((tm // 2, K), lambda i, j: (2 * i, 0)),
            pl.BlockSpec((tm // 2, K), lambda i, j: (2 * i + 1, 0), pipeline_mode=pl.Buffered(1)),
            pl.BlockSpec((pl.Element(tn), pl.Element(K)), w_row),
            pl.BlockSpec((GATE_W, K), lambda i, j: (HEAD_W // GATE_W, 0)),
        ],
        out_specs=(
            pl.BlockSpec((tm, tn), lambda i, j: (i, j)),
            pl.BlockSpec((tm, GATE_W), lambda i, j: (i, 0)),
        ),
        compiler_params=pltpu.CompilerParams(
            dimension_semantics=("arbitrary", "arbitrary"),
            vmem_limit_bytes=V7X_VMEM_LIMIT),
        name="in_proj",
    )(x_bf, x_bf, w_t, w_t)


def _t5_bucket_np(dist):
    max_exact = NUM_BUCKETS // 2
    is_small = dist < max_exact
    ratio = np.maximum(dist, max_exact).astype(np.float32) / np.float32(max_exact)
    large = max_exact + (np.log(ratio) / np.float32(math.log(MAX_DISTANCE / max_exact))
                         * np.float32(NUM_BUCKETS - max_exact)).astype(np.int32)
    large = np.minimum(large, NUM_BUCKETS - 1)
    return np.where(is_small, dist, large)


def _swa_bucket_row():
    c = np.arange(2 * BLOCK, dtype=np.int32)
    return _t5_bucket_np(np.clip(BLOCK - c, 0, MAX_DISTANCE - 1)).astype(np.int32)[None, :]


def _swa_build_bias(bucket_ref, relb_ref, tbl_ref):
    q = lax.broadcasted_iota(jnp.int32, (BLOCK, 2 * BLOCK), 0)
    s = lax.broadcasted_iota(jnp.int32, (BLOCK, 2 * BLOCK), 1)
    dist = q + BLOCK - s
    in_win = jnp.where(dist >= 0, jnp.where(dist < WINDOW, 1, 0), 0)
    bucket = bucket_ref[...]
    for h in range(SWA_HEADS):
        u = jnp.zeros((1, 2 * BLOCK), F32)
        for bkt in range(NUM_BUCKETS):
            u = jnp.where(bucket == bkt, relb_ref[bkt * SWA_HEADS + h], u)
        t = pltpu.roll(jnp.broadcast_to(u, (BLOCK, 2 * BLOCK)), 0, 1, stride=1, stride_axis=0)
        tbl_ref[h] = jnp.where(in_win == 1, t, -jnp.inf)


def _swa_block(sink_ref, q_ref, z0_ref, z1_ref, kp_ref, kc_ref, vp_ref, vc_ref, tbl_ref, first_mask, out,
               kv_heads):
    scale = SWA_HEAD_DIM ** -0.5
    half = SWA_HEADS // 2
    for kv in kv_heads:
        ksl = slice(kv * SWA_HEAD_DIM, (kv + 1) * SWA_HEAD_DIM)
        kk = jnp.concatenate([kp_ref[:, ksl], kc_ref[:, ksl]], axis=0)
        vv = jnp.concatenate([vp_ref[:, ksl], vc_ref[:, ksl]], axis=0)
        heads = [kv * SWA_GROUP + g for g in range(SWA_GROUP)]
        q4 = jnp.concatenate(
            [q_ref[:, h * SWA_HEAD_DIM:(h + 1) * SWA_HEAD_DIM] for h in heads], axis=0)
        s4 = lax.dot_general(q4, kk, _NT, preferred_element_type=F32)
        ps, denoms = [], []
        for g, h in enumerate(heads):
            s = s4[g * BLOCK:(g + 1) * BLOCK] * scale + (tbl_ref[h] + first_mask)
            sink = sink_ref[h]
            m = jnp.maximum(jnp.max(s, axis=-1, keepdims=True), sink)
            p = jnp.exp(s - m)
            denoms.append(jnp.sum(p, axis=-1, keepdims=True) + jnp.exp(sink - m))
            ps.append(p.astype(BF16))
        o4 = jnp.dot(jnp.concatenate(ps, axis=0), vv, preferred_element_type=F32)
        for g, h in enumerate(heads):
            z_ref = z0_ref if h < half else z1_ref
            zsl = slice((h % half) * SWA_HEAD_DIM, (h % half + 1) * SWA_HEAD_DIM)
            o = o4[g * BLOCK:(g + 1) * BLOCK] / denoms[g]
            out[:, h * SWA_HEAD_DIM:(h + 1) * SWA_HEAD_DIM] = (
                o * _silu(z_ref[:, zsl].astype(F32))).astype(out.dtype)


def _cumsum_lanes(x):
    n = x.shape[-1]
    lane = lax.broadcasted_iota(jnp.int32, x.shape, x.ndim - 1)
    shift = 1
    while shift < n:
        x = x + jnp.where(lane >= shift, pltpu.roll(x, shift, x.ndim - 1), 0.0)
        shift *= 2
    return x


_CONV_HALO = 8


def _mlstm_reset(xbuf, c_state, n_state, m_state):
    xbuf[0:_CONV_HALO, :] = jnp.zeros((_CONV_HALO, 2 * M_QK_W), F32)
    c_state[...] = jnp.zeros_like(c_state)
    n_state[...] = jnp.zeros_like(n_state)
    m_state[...] = jnp.zeros_like(m_state)


def _mlstm_chunk(q_ref, k_ref, v_ref, g_ref, o_ref, z_ref, cw_ref, cb_ref, gb_ref, ng_ref,
                 xbuf, c_state, n_state, m_state, out, col0):
    L = M_CHUNK
    HALO = _CONV_HALO

    xbuf[HALO:HALO + L, 0:M_QK_W] = q_ref[...].astype(F32)
    xbuf[HALO:HALO + L, M_QK_W:] = k_ref[...].astype(F32)
    acc = cb_ref[...] + cw_ref[CONV_WIDTH - 1:CONV_WIDTH, :] * xbuf[HALO:HALO + L, :]
    for j in range(CONV_WIDTH - 1):
        back = CONV_WIDTH - 1 - j
        acc = acc + cw_ref[j:j + 1, :] * xbuf[HALO - back:HALO - back + L, :]
    tail = xbuf[L:L + HALO, :]
    xbuf[0:HALO, :] = tail
    qk = _silu(acc)
    q_all = qk[:, 0:M_QK_W]
    k_all = qk[:, M_QK_W:] * (M_QK_DIM ** -0.5)

    gt = (g_ref[...] + gb_ref[...]).T[0:8, :]
    row8 = lax.broadcasted_iota(jnp.int32, (8, L), 0)
    head_rows = row8 >= M_HEADS
    log_i = jnp.where(head_rows, pltpu.roll(gt, M_HEADS, 0), 0.0)
    log_f = jnp.where(head_rows, _log_sigmoid(gt), 0.0)
    b = _cumsum_lanes(log_f)
    g_tot = b[:, L - 1:L]
    a = g_tot - b + log_i
    m_loc = jnp.max(a, axis=-1, keepdims=True)
    w = jnp.exp(a - m_loc)
    m_prev = m_state[...]
    inter = b + m_prev
    r = log_i - b

    cols = jnp.concatenate([inter, w, b, jnp.zeros((L - 24, L), F32)], axis=0).T

    g_rep = jnp.broadcast_to(g_tot, (8, L))
    m_loc_rep = jnp.broadcast_to(m_loc, (8, L))
    m_new = jnp.maximum(g_rep + m_prev, m_loc_rep)
    s_prev = jnp.exp(g_rep + m_prev - m_new)
    s_loc = jnp.exp(m_loc_rep - m_new)
    n_prev_all = n_state[...]

    ti = lax.broadcasted_iota(jnp.int32, (L, L), 0)
    si = lax.broadcasted_iota(jnp.int32, (L, L), 1)
    causal = ti >= si

    for h in range(M_HEADS):
        rr = M_HEADS + h
        qh = q_all[:, h * M_QK_DIM:(h + 1) * M_QK_DIM]
        kh = k_all[:, h * M_QK_DIM:(h + 1) * M_QK_DIM]
        vsl = slice(h * M_V_DIM, (h + 1) * M_V_DIM)
        vh = v_ref[:, vsl]
        inter_c = cols[:, rr:rr + 1]
        w_c = cols[:, 8 + rr:9 + rr]
        b_c = cols[:, 16 + rr:17 + rr]

        dmat = jnp.where(causal, b_c + r[rr:rr + 1, :], -jnp.inf)
        m_t = jnp.maximum(inter_c, jnp.max(dmat, axis=-1, keepdims=True))
        qb = qh.astype(BF16)
        kb = kh.astype(BF16)
        s = lax.dot_general(qb, kb, _NT, preferred_element_type=F32)
        sm = s * jnp.exp(dmat - m_t)
        w_inter = jnp.exp(inter_c - m_t)
        c_prev = c_state[h]
        num = (jnp.dot(sm.astype(BF16), vh, preferred_element_type=F32)
               + w_inter * jnp.dot(qb, c_prev.astype(BF16), preferred_element_type=F32))
        n_prev = n_prev_all[rr:rr + 1, :]
        den = (jnp.sum(sm, axis=-1, keepdims=True)
               + w_inter * jnp.sum(qh * n_prev, axis=-1, keepdims=True))
        hcell = num / jnp.maximum(jnp.abs(den), jnp.exp(-m_t))

        hg = _sigmoid(o_ref[:, vsl].astype(F32)) * hcell
        mu = jnp.mean(hg, axis=-1, keepdims=True)
        var = jnp.mean(jnp.square(hg - mu), axis=-1, keepdims=True)
        hn = (hg - mu) * lax.rsqrt(var + HEAD_NORM_EPS) * ng_ref[:, vsl]
        out[:, col0 + h * M_V_DIM:col0 + (h + 1) * M_V_DIM] = (
            hn * _silu(z_ref[:, vsl].astype(F32))).astype(out.dtype)

        wk = w_c * kh
        c_loc = jnp.dot(wk.T.astype(BF16), vh, preferred_element_type=F32)
        sp_row = s_prev[rr:rr + 1, :]
        sl_row = s_loc[rr:rr + 1, :]
        sp2 = jnp.concatenate([sp_row, sp_row], axis=1)
        sl2 = jnp.concatenate([sl_row, sl_row], axis=1)
        c_state[h] = sp2 * c_prev + sl2 * c_loc
        n_loc = jnp.sum(wk, axis=0, keepdims=True)
        n_state[rr:rr + 1, :] = sp_row * n_prev + sl_row * n_loc

    m_state[...] = m_new


def _mem_kv_kernel(mem_ref, w_ref, o_ref):
    o_ref[...] = jnp.dot(mem_ref[...].astype(BF16), w_ref[...].astype(BF16),
                         preferred_element_type=F32).astype(o_ref.dtype)


def _mem_kv(mem2d, w_mem_kv, *, tn=512):
    M, K = mem2d.shape
    N = w_mem_kv.shape[1]
    return pl.pallas_call(
        _mem_kv_kernel,
        out_shape=jax.ShapeDtypeStruct((M, N), BF16),
        grid=(N // tn,),
        in_specs=[pl.BlockSpec((M, K), lambda j: (0, 0)),
                  pl.BlockSpec((K, tn), lambda j: (0, j))],
        out_specs=pl.BlockSpec((M, tn), lambda j: (0, j)),
        compiler_params=pltpu.CompilerParams(
            dimension_semantics=("arbitrary",), vmem_limit_bytes=V7X_VMEM_LIMIT),
        name="mem_kv",
    )(mem2d, w_mem_kv)


def _xattn_block(q_ref, z_ref, k_ref, v_ref, out, col0):
    scale = X_HEAD_DIM ** -0.5
    for h in range(X_HEADS):
        hsl = slice(h * X_HEAD_DIM, (h + 1) * X_HEAD_DIM)
        s = lax.dot_general(q_ref[:, hsl], k_ref[:, hsl], _NT, preferred_element_type=F32) * scale
        m = jnp.max(s, axis=-1, keepdims=True)
        p = jnp.exp(s - m)
        denom = jnp.sum(p, axis=-1, keepdims=True)
        o = jnp.dot(p.astype(BF16), v_ref[:, hsl], preferred_element_type=F32) / denom
        out[:, col0 + h * X_HEAD_DIM:col0 + (h + 1) * X_HEAD_DIM] = (
            o * _silu(z_ref[:, hsl].astype(F32))).astype(out.dtype)


_LANES = 128


def _fused_kernel(sink_ref, relb_ref, bucket_ref,
                  cur_ref, kvp_ref, g_ref, cw_ref, cb_ref, gb_ref, ng_ref, mem_ref,
                  w_ref, x_ref, xc_ref, lng_ref, lnb_ref,
                  o_ref,
                  mix_cur, mix_new, rbuf, s1, s2, mean_s, rstd_s, tbl, xbuf, c_state, n_state, m_state,
                  *, n_tiles, blocks_per_seq):
    i, j = pl.program_id(0), pl.program_id(1)
    nj = pl.num_programs(1)
    tm, tn = x_ref.shape
    d_model = rbuf.shape[1]
    blk = jnp.minimum(i, n_tiles - 1) * nj + j
    col = pl.multiple_of(j * tn, tn)

    seg = lambda name: cur_ref.at[:, _proj_cols(name)]
    q_ref, kc_ref, vc_ref = seg("a_q"), seg("a_k"), seg("a_v")
    az = _proj_cols("a_z")
    z0_ref = cur_ref.at[:, az.start:az.start + SWA_W // 2]
    z1_ref = cur_ref.at[:, az.start + SWA_W // 2:az.stop]
    kp_ref, vp_ref = kvp_ref.at[:, 0:SWA_KV_W], kvp_ref.at[:, SWA_KV_W:2 * SWA_KV_W]
    mq_ref, mk_ref, mv_ref, mo_ref, mz_ref = seg("m_q"), seg("m_k"), seg("m_v"), seg("m_o"), seg("m_z")
    cq_ref, cz_ref = seg("c_q"), seg("c_z")
    memk_ref, memv_ref = mem_ref.at[:, 0:X_W], mem_ref.at[:, X_W:2 * X_W]

    @pl.when((i == 0) & (j == 0))
    def _():
        _swa_build_bias(bucket_ref, relb_ref, tbl)
        mix_cur[...] = jnp.zeros_like(mix_cur)
        mix_new[...] = jnp.zeros_like(mix_new)
        rbuf[...] = jnp.zeros_like(rbuf)
        s1[...] = jnp.zeros_like(s1)
        s2[...] = jnp.zeros_like(s2)
        mean_s[...] = jnp.zeros_like(mean_s)
        rstd_s[...] = jnp.zeros_like(rstd_s)

    @pl.when((i < n_tiles) & (blk % blocks_per_seq == 0))
    def _():
        _mlstm_reset(xbuf, c_state, n_state, m_state)

    def shift():
        return jnp.broadcast_to(DEEPNORM_ALPHA * xc_ref[:, 0:1], (tm, _LANES))

    def stage_c():
        mean, rstd = mean_s[...], rstd_s[...]
        for c in range(tn // _LANES):
            sl = slice(c * _LANES, (c + 1) * _LANES)
            dsl = pl.ds(col + c * _LANES, _LANES)
            o_ref[:, sl] = (rbuf[:, dsl] - mean) * rstd * lng_ref[:, dsl] + lnb_ref[:, dsl]

    def stage_b(c, n_chunks):
        rh = tm // n_chunks
        rsl = slice(c * rh, (c + 1) * rh)
        y = jnp.dot(mix_cur[rsl, :], w_ref[...], preferred_element_type=F32)
        r = DEEPNORM_ALPHA * x_ref[rsl, :] + y
        rbuf[rsl, pl.ds(col, tn)] = r
        c0 = DEEPNORM_ALPHA * jnp.broadcast_to(xc_ref[rsl, 0:1], (rh, _LANES))
        a1, a2 = s1[rsl, :], s2[rsl, :]
        for l in range(tn // _LANES):
            d = r[:, l * _LANES:(l + 1) * _LANES] - c0
            a1 = a1 + d
            a2 = a2 + d * d
        s1[rsl, :] = a1
        s2[rsl, :] = a2

    def stage_a_pieces():
        out = mix_new.at[pl.ds(pl.multiple_of(j * BLOCK, BLOCK), BLOCK)]
        lane = lax.broadcasted_iota(jnp.int32, (1, 2 * BLOCK), 1)
        neg = jnp.where(blk % blocks_per_seq == 0, -jnp.inf, 0.0)
        first_mask = jnp.where(lane < BLOCK, neg, 0.0)
        swa = functools.partial(_swa_block, sink_ref, q_ref, z0_ref, z1_ref, kp_ref, kc_ref, vp_ref, vc_ref,
                                tbl, first_mask, out)
        half = SWA_KV_HEADS // 2
        return [
            lambda: swa(range(0, half)),
            lambda: swa(range(half, SWA_KV_HEADS)),
            lambda: _mlstm_chunk(mq_ref, mk_ref, mv_ref, g_ref, mo_ref, mz_ref, cw_ref, cb_ref, gb_ref, ng_ref,
                                 xbuf, c_state, n_state, m_state, out, SWA_W),
            lambda: _xattn_block(cq_ref, cz_ref, memk_ref, memv_ref, out, SWA_W + M_V_W),
        ]

    @pl.when(i < n_tiles)
    def _():
        stage_c()
        pieces = stage_a_pieces()
        for c, piece in enumerate(pieces):
            piece()
            stage_b(c, len(pieces))

    @pl.when(i == n_tiles)
    def _():
        stage_c()
        stage_b(0, 1)

    @pl.when(i == n_tiles + 1)
    def _():
        stage_c()

    @pl.when(j == nj - 1)
    def _():
        mean_d = jnp.sum(s1[...], axis=-1, keepdims=True) * (1.0 / d_model)
        var = jnp.sum(s2[...], axis=-1, keepdims=True) * (1.0 / d_model) - mean_d * mean_d
        mean_s[...] = shift() + jnp.broadcast_to(mean_d, (tm, _LANES))
        rstd_s[...] = jnp.broadcast_to(lax.rsqrt(var + LN_EPS), (tm, _LANES))
        s1[...] = jnp.zeros_like(s1)
        s2[...] = jnp.zeros_like(s2)
        mix_cur[...] = mix_new[...]


def _fused_mix_out(proj, gates, mkv, w_out_bf, x2d, rel_bias, sinks, conv_w, conv_b, gate_bias, norm_g,
                   ln_g, ln_b, *, batch, seq, tm=512, tn=1024):
    T, D = x2d.shape
    n_tiles = T // tm
    nj = D // tn
    blocks_per_seq = seq // BLOCK
    assert tm // BLOCK == nj and T % tm == 0 and D % tn == 0 and M_CHUNK == BLOCK and MIX_W == D

    def blk(i, j):
        return jnp.minimum(i, n_tiles - 1) * nj + j

    def cur(col):
        return lambda i, j, *_: (blk(i, j), col)

    def prev(col):
        return lambda i, j, *_: (jnp.maximum(blk(i, j) - 1, 0), col)

    def mem(col):
        return lambda i, j, *_: (blk(i, j) // blocks_per_seq, col)

    def const(i, j, *_):
        return (0, 0)

    def row_b(i):
        return jnp.clip(i - 1, 0, n_tiles - 1)

    kv = _proj_cols("a_k")
    assert _proj_cols("a_v").start == kv.stop and kv.start % (2 * SWA_KV_W) == 0
    in_specs = [
        pl.BlockSpec((1, 2 * BLOCK), const),
        pl.BlockSpec((BLOCK, PROJ_W), cur(0)),
        pl.BlockSpec((BLOCK, 2 * SWA_KV_W), prev(kv.start // (2 * SWA_KV_W))),
        pl.BlockSpec((BLOCK, GATE_W), cur(0)),
        pl.BlockSpec((CONV_WIDTH, 2 * M_QK_W), const),
        pl.BlockSpec((1, 2 * M_QK_W), const),
        pl.BlockSpec((1, GATE_W), const),
        pl.BlockSpec((1, M_V_W), const),
        pl.BlockSpec((MEM_LEN, 2 * X_W), mem(0), pipeline_mode=pl.Buffered(1)),
        pl.BlockSpec((D, tn), lambda i, j, *_: (0, j)),
        pl.BlockSpec((tm, tn), lambda i, j, *_: (row_b(i), j)),
        pl.BlockSpec((tm, _LANES), lambda i, j, *_: (row_b(i), 0)),
        pl.BlockSpec((1, D), const),
        pl.BlockSpec((1, D), const),
    ]
    grid_spec = pltpu.PrefetchScalarGridSpec(
        num_scalar_prefetch=2,
        grid=(n_tiles + 2, nj),
        in_specs=in_specs,
        out_specs=pl.BlockSpec((tm, tn), lambda i, j, *_: (jnp.maximum(i - 2, 0), jnp.where(i < 2, 0, j))),
        scratch_shapes=[
            pltpu.VMEM((tm, D), BF16),
            pltpu.VMEM((tm, D), BF16),
            pltpu.VMEM((tm, D), F32),
            pltpu.VMEM((tm, _LANES), F32),
            pltpu.VMEM((tm, _LANES), F32),
            pltpu.VMEM((tm, _LANES), F32),
            pltpu.VMEM((tm, _LANES), F32),
            pltpu.VMEM((SWA_HEADS, BLOCK, 2 * BLOCK), F32),
            pltpu.VMEM((M_CHUNK + _CONV_HALO, 2 * M_QK_W), F32),
            pltpu.VMEM((M_HEADS, M_QK_DIM, M_V_DIM), F32),
            pltpu.VMEM((8, M_QK_DIM), F32),
            pltpu.VMEM((8, M_CHUNK), F32),
        ],
    )
    return pl.pallas_call(
        functools.partial(_fused_kernel, n_tiles=n_tiles, blocks_per_seq=blocks_per_seq),
        out_shape=jax.ShapeDtypeStruct((T, D), F32),
        grid_spec=grid_spec,
        compiler_params=pltpu.CompilerParams(
            dimension_semantics=("arbitrary", "arbitrary"), vmem_limit_bytes=V7X_VMEM_LIMIT),
        name="mix_out",
    )(sinks, rel_bias.reshape(-1), jnp.asarray(_swa_bucket_row()),
      proj, proj, gates, conv_w, conv_b, gate_bias, norm_g, mkv,
      w_out_bf, x2d, x2d, ln_g, ln_b)


def kernel(x, mem, w_in, conv_w, conv_b, b_i, b_f, m_norm_g, rel_bias, sinks, w_mem_kv, w_out, ln_g, ln_b):
    B, S, D = x.shape
    assert D == D_MODEL and S % BLOCK == 0 and S % M_CHUNK == 0
    T = B * S
    x2d = x.reshape(T, D)
    gate_bias = jnp.concatenate(
        [b_i.astype(F32), b_f.astype(F32), jnp.zeros((GATE_W - GATE_COLS,), F32)])[None, :]

    proj, gates = _in_proj(x2d.astype(BF16), w_in.T)
    mkv = _mem_kv(mem.reshape(B * MEM_LEN, D), w_mem_kv)
    out = _fused_mix_out(proj, gates, mkv, w_out.astype(BF16), x2d, rel_bias.astype(F32), sinks.astype(F32),
                         conv_w.astype(F32), conv_b.astype(F32)[None, :], gate_bias,
                         m_norm_g.astype(F32)[None, :], ln_g.astype(F32)[None, :], ln_b.astype(F32)[None, :],
                         batch=B, seq=S)
    return out.reshape(B, S, D).astype(x.dtype)
```

```python
import functools
import math

import jax
import jax.numpy as jnp
import numpy as np
from jax import lax
from jax.experimental import pallas as pl
from jax.experimental.pallas import tpu as pltpu

F32 = jnp.float32
BF16 = jnp.bfloat16

D_MODEL = 4096
MEM_LEN = 256
SWA_HEADS = 16
SWA_KV_HEADS = 4
SWA_GROUP = SWA_HEADS // SWA_KV_HEADS
SWA_HEAD_DIM = 128
WINDOW = 128
BLOCK = 128
M_HEADS = 4
M_QK_DIM = 128
M_V_DIM = 256
M_CHUNK = 128
CONV_WIDTH = 4
X_HEADS = 4
X_HEAD_DIM = 256
NUM_BUCKETS = 32
MAX_DISTANCE = 128

SWA_W = SWA_HEADS * SWA_HEAD_DIM
SWA_KV_W = SWA_KV_HEADS * SWA_HEAD_DIM
M_QK_W = M_HEADS * M_QK_DIM
M_V_W = M_HEADS * M_V_DIM
X_W = X_HEADS * X_HEAD_DIM
MIX_W = SWA_W + M_V_W + X_W

DEEPNORM_ALPHA = 2.0 ** 0.25
LN_EPS = 1e-5
HEAD_NORM_EPS = 1e-6

_REF_SPLITS = (
    ("a_q", SWA_W), ("a_k", SWA_KV_W), ("a_v", SWA_KV_W), ("a_z", SWA_W),
    ("m_q", M_QK_W), ("m_k", M_QK_W), ("m_v", M_V_W), ("m_i", M_HEADS), ("m_f", M_HEADS),
    ("m_o", M_V_W), ("m_z", M_V_W), ("c_q", X_W), ("c_z", X_W),
)
GATE_W = 128
V7X_VMEM_BYTES = 64 * 1024 * 1024
V7X_VMEM_LIMIT = V7X_VMEM_BYTES - 4 * 1024 * 1024


def _ref_offsets():
    offs, acc = {}, 0
    for name, width in _REF_SPLITS:
        offs[name] = acc
        acc += width
    return offs


_REF_OFFS = _ref_offsets()
HEAD_W = _REF_OFFS["m_i"]
GATE_COLS = 2 * M_HEADS
TAIL_START = HEAD_W + GATE_COLS
TAIL_W = 2 * M_V_W + 2 * X_W
PROJ_W = HEAD_W + TAIL_W
_HEAD_SEGS = ("a_q", "a_k", "a_v", "a_z", "m_q", "m_k", "m_v")


def _proj_cols(name):
    off = _REF_OFFS[name] - (0 if name in _HEAD_SEGS else GATE_COLS)
    return slice(off, off + dict(_REF_SPLITS)[name])


def _silu(z):
    return z * (1.0 / (1.0 + jnp.exp(-z)))


def _sigmoid(z):
    return 1.0 / (1.0 + jnp.exp(-z))


def _log_sigmoid(z):
    return jnp.minimum(z, 0.0) - jnp.log1p(jnp.exp(-jnp.abs(z)))


_NT = (((1,), (1,)), ((), ()))


def _in_proj_kernel(xa_ref, xb_ref, w_ref, wg_ref, o_ref, g_ref):
    half = xa_ref.shape[0]
    w = w_ref[...].astype(BF16)
    for r, x_ref in enumerate((xa_ref, xb_ref)):
        rows = slice(r * half, (r + 1) * half)
        o_ref[rows, :] = lax.dot_general(x_ref[...], w, _NT, preferred_element_type=F32).astype(o_ref.dtype)

    @pl.when(pl.program_id(1) == 0)
    def _():
        wg = wg_ref[...].astype(BF16)
        for r, x_ref in enumerate((xa_ref, xb_ref)):
            rows = slice(r * half, (r + 1) * half)
            g_ref[rows, :] = lax.dot_general(x_ref[...], wg, _NT, preferred_element_type=F32)


def _in_proj(x_bf, w_t, *, tm=2048, tn=512):
    T, K = x_bf.shape
    n_head, n_tail = HEAD_W // tn, TAIL_W // tn
    assert HEAD_W % tn == 0 and TAIL_W % tn == 0 and HEAD_W % GATE_W == 0

    def w_row(i, j):
        row = jnp.where(j < n_head, j * tn, TAIL_START + (j - n_head) * tn)
        return pl.multiple_of(row, math.gcd(tn, TAIL_START)), 0

    return pl.pallas_call(
        _in_proj_kernel,
        out_shape=(jax.ShapeDtypeStruct((T, PROJ_W), BF16), jax.ShapeDtypeStruct((T, GATE_W), F32)),
        grid=(T // tm, n_head + n_tail),
        in_specs=[
            pl.BlockSpec((tm // 2, K), lambda i, j: (2 * i, 0)),
            pl.BlockSpec((tm // 2, K), lambda i, j: (2 * i + 1, 0), pipeline_mode=pl.Buffered(1)),
            pl.BlockSpec((pl.Element(tn), pl.Element(K)), w_row),
            pl.BlockSpec((GATE_W, K), lambda i, j: (HEAD_W // GATE_W, 0)),
        ],
        out_specs=(
            pl.BlockSpec((tm, tn), lambda i, j: (i, j)),
            pl.BlockSpec((tm, GATE_W), lambda i, j: (i, 0)),
        ),
        compiler_params=pltpu.CompilerParams(
            dimension_semantics=("arbitrary", "arbitrary"),
            vmem_limit_bytes=V7X_VMEM_LIMIT),
        name="in_proj",
    )(x_bf, x_bf, w_t, w_t)


def _t5_bucket_np(dist):
    max_exact = NUM_BUCKETS // 2
    is_small = dist < max_exact
    ratio = np.maximum(dist, max_exact).astype(np.float32) / np.float32(max_exact)
    large = max_exact + (np.log(ratio) / np.float32(math.log(MAX_DISTANCE / max_exact))
                         * np.float32(NUM_BUCKETS - max_exact)).astype(np.int32)
    large = np.minimum(large, NUM_BUCKETS - 1)
    return np.where(is_small, dist, large)


def _swa_bucket_row():
    c = np.arange(2 * BLOCK, dtype=np.int32)
    return _t5_bucket_np(np.clip(BLOCK - c, 0, MAX_DISTANCE - 1)).astype(np.int32)[None, :]


def _swa_build_bias(bucket_ref, relb_ref, tbl_ref):
    q = lax.broadcasted_iota(jnp.int32, (BLOCK, 2 * BLOCK), 0)
    s = lax.broadcasted_iota(jnp.int32, (BLOCK, 2 * BLOCK), 1)
    dist = q + BLOCK - s
    in_win = jnp.where(dist >= 0, jnp.where(dist < WINDOW, 1, 0), 0)
    bucket = bucket_ref[...]
    for h in range(SWA_HEADS):
        u = jnp.zeros((1, 2 * BLOCK), F32)
        for bkt in range(NUM_BUCKETS):
            u = jnp.where(bucket == bkt, relb_ref[bkt * SWA_HEADS + h], u)
        t = pltpu.roll(jnp.broadcast_to(u, (BLOCK, 2 * BLOCK)), 0, 1, stride=1, stride_axis=0)
        tbl_ref[h] = jnp.where(in_win == 1, t, -jnp.inf)


_SWA_YIELDS_PER_KV = 2


def _swa_block(sink_ref, q_ref, z0_ref, z1_ref, kp_ref, kc_ref, vp_ref, vc_ref, tbl_ref, first_mask, out,
               kv_heads):
    scale = SWA_HEAD_DIM ** -0.5
    half = SWA_HEADS // 2
    for kv in kv_heads:
        ksl = slice(kv * SWA_HEAD_DIM, (kv + 1) * SWA_HEAD_DIM)
        kk = jnp.concatenate([kp_ref[:, ksl], kc_ref[:, ksl]], axis=0)
        vv = jnp.concatenate([vp_ref[:, ksl], vc_ref[:, ksl]], axis=0)
        heads = [kv * SWA_GROUP + g for g in range(SWA_GROUP)]
        q4 = jnp.concatenate(
            [q_ref[:, h * SWA_HEAD_DIM:(h + 1) * SWA_HEAD_DIM] for h in heads], axis=0)
        s4 = lax.dot_general(q4, kk, _NT, preferred_element_type=F32)
        ps, denoms = [], []
        for g, h in enumerate(heads):
            s = s4[g * BLOCK:(g + 1) * BLOCK] * scale + (tbl_ref[h] + first_mask)
            sink = sink_ref[h]
            m = jnp.maximum(jnp.max(s, axis=-1, keepdims=True), sink)
            p = jnp.exp(s - m)
            denoms.append(jnp.sum(p, axis=-1, keepdims=True) + jnp.exp(sink - m))
            ps.append(p.astype(BF16))
        yield
        o4 = jnp.dot(jnp.concatenate(ps, axis=0), vv, preferred_element_type=F32)
        for g, h in enumerate(heads):
            z_ref = z0_ref if h < half else z1_ref
            zsl = slice((h % half) * SWA_HEAD_DIM, (h % half + 1) * SWA_HEAD_DIM)
            o = o4[g * BLOCK:(g + 1) * BLOCK] / denoms[g]
            out[:, h * SWA_HEAD_DIM:(h + 1) * SWA_HEAD_DIM] = (
                o * _silu(z_ref[:, zsl].astype(F32))).astype(out.dtype)
        yield


def _cumsum_lanes(x):
    n = x.shape[-1]
    lane = lax.broadcasted_iota(jnp.int32, x.shape, x.ndim - 1)
    shift = 1
    while shift < n:
        x = x + jnp.where(lane >= shift, pltpu.roll(x, shift, x.ndim - 1), 0.0)
        shift *= 2
    return x


_CONV_HALO = 8


def _mlstm_reset(xbuf, c_state, n_state, m_state):
    xbuf[0:_CONV_HALO, :] = jnp.zeros((_CONV_HALO, 2 * M_QK_W), F32)
    c_state[...] = jnp.zeros_like(c_state)
    n_state[...] = jnp.zeros_like(n_state)
    m_state[...] = jnp.zeros_like(m_state)


_MLSTM_YIELDS = 1 + 3 * M_HEADS


def _mlstm_chunk(q_ref, k_ref, v_ref, g_ref, o_ref, z_ref, cw_ref, cb_ref, gb_ref, ng_ref,
                 xbuf, c_state, n_state, m_state, out, col0):
    L = M_CHUNK
    HALO = _CONV_HALO

    xbuf[HALO:HALO + L, 0:M_QK_W] = q_ref[...].astype(F32)
    xbuf[HALO:HALO + L, M_QK_W:] = k_ref[...].astype(F32)
    acc = cb_ref[...] + cw_ref[CONV_WIDTH - 1:CONV_WIDTH, :] * xbuf[HALO:HALO + L, :]
    for j in range(CONV_WIDTH - 1):
        back = CONV_WIDTH - 1 - j
        acc = acc + cw_ref[j:j + 1, :] * xbuf[HALO - back:HALO - back + L, :]
    tail = xbuf[L:L + HALO, :]
    xbuf[0:HALO, :] = tail
    qk = _silu(acc)
    q_all = qk[:, 0:M_QK_W]
    k_all = qk[:, M_QK_W:] * (M_QK_DIM ** -0.5)

    gt = (g_ref[...] + gb_ref[...]).T[0:8, :]
    row8 = lax.broadcasted_iota(jnp.int32, (8, L), 0)
    head_rows = row8 >= M_HEADS
    log_i = jnp.where(head_rows, pltpu.roll(gt, M_HEADS, 0), 0.0)
    log_f = jnp.where(head_rows, _log_sigmoid(gt), 0.0)
    b = _cumsum_lanes(log_f)
    g_tot = b[:, L - 1:L]
    a = g_tot - b + log_i
    m_loc = jnp.max(a, axis=-1, keepdims=True)
    w = jnp.exp(a - m_loc)
    m_prev = m_state[...]
    inter = b + m_prev
    r = log_i - b

    cols = jnp.concatenate([inter, w, b, jnp.zeros((L - 24, L), F32)], axis=0).T

    g_rep = jnp.broadcast_to(g_tot, (8, L))
    m_loc_rep = jnp.broadcast_to(m_loc, (8, L))
    m_new = jnp.maximum(g_rep + m_prev, m_loc_rep)
    s_prev = jnp.exp(g_rep + m_prev - m_new)
    s_loc = jnp.exp(m_loc_rep - m_new)
    n_prev_all = n_state[...]

    ti = lax.broadcasted_iota(jnp.int32, (L, L), 0)
    si = lax.broadcasted_iota(jnp.int32, (L, L), 1)
    causal = ti >= si
    yield

    for h in range(M_HEADS):
        rr = M_HEADS + h
        qh = q_all[:, h * M_QK_DIM:(h + 1) * M_QK_DIM]
        kh = k_all[:, h * M_QK_DIM:(h + 1) * M_QK_DIM]
        vsl = slice(h * M_V_DIM, (h + 1) * M_V_DIM)
        vh = v_ref[:, vsl]
        inter_c = cols[:, rr:rr + 1]
        w_c = cols[:, 8 + rr:9 + rr]
        b_c = cols[:, 16 + rr:17 + rr]

        dmat = jnp.where(causal, b_c + r[rr:rr + 1, :], -jnp.inf)
        m_t = jnp.maximum(inter_c, jnp.max(dmat, axis=-1, keepdims=True))
        qb = qh.astype(BF16)
        kb = kh.astype(BF16)
        s = lax.dot_general(qb, kb, _NT, preferred_element_type=F32)
        sm = s * jnp.exp(dmat - m_t)
        w_inter = jnp.exp(inter_c - m_t)
        yield
        c_prev = c_state[h]
        num = (jnp.dot(sm.astype(BF16), vh, preferred_element_type=F32)
               + w_inter * jnp.dot(qb, c_prev.astype(BF16), preferred_element_type=F32))
        n_prev = n_prev_all[rr:rr + 1, :]
        den = (jnp.sum(sm, axis=-1, keepdims=True)
               + w_inter * jnp.sum(qh * n_prev, axis=-1, keepdims=True))
        hcell = num / jnp.maximum(jnp.abs(den), jnp.exp(-m_t))

        hg = _sigmoid(o_ref[:, vsl].astype(F32)) * hcell
        mu = jnp.mean(hg, axis=-1, keepdims=True)
        var = jnp.mean(jnp.square(hg - mu), axis=-1, keepdims=True)
        hn = (hg - mu) * lax.rsqrt(var + HEAD_NORM_EPS) * ng_ref[:, vsl]
        out[:, col0 + h * M_V_DIM:col0 + (h + 1) * M_V_DIM] = (
            hn * _silu(z_ref[:, vsl].astype(F32))).astype(out.dtype)
        yield

        wk = w_c * kh
        c_loc = jnp.dot(wk.T.astype(BF16), vh, preferred_element_type=F32)
        sp_row = s_prev[rr:rr + 1, :]
        sl_row = s_loc[rr:rr + 1, :]
        sp2 = jnp.concatenate([sp_row, sp_row], axis=1)
        sl2 = jnp.concatenate([sl_row, sl_row], axis=1)
        c_state[h] = sp2 * c_prev + sl2 * c_loc
        n_loc = jnp.sum(wk, axis=0, keepdims=True)
        n_state[rr:rr + 1, :] = sp_row * n_prev + sl_row * n_loc
        yield

    m_state[...] = m_new


def _mem_kv_kernel(mem_ref, w_ref, o_ref):
    o_ref[...] = jnp.dot(mem_ref[...].astype(BF16), w_ref[...].astype(BF16),
                         preferred_element_type=F32).astype(o_ref.dtype)


def _mem_kv(mem2d, w_mem_kv, *, tn=512):
    M, K = mem2d.shape
    N = w_mem_kv.shape[1]
    return pl.pallas_call(
        _mem_kv_kernel,
        out_shape=jax.ShapeDtypeStruct((M, N), BF16),
        grid=(N // tn,),
        in_specs=[pl.BlockSpec((M, K), lambda j: (0, 0)),
                  pl.BlockSpec((K, tn), lambda j: (0, j))],
        out_specs=pl.BlockSpec((M, tn), lambda j: (0, j)),
        compiler_params=pltpu.CompilerParams(
            dimension_semantics=("arbitrary",), vmem_limit_bytes=V7X_VMEM_LIMIT),
        name="mem_kv",
    )(mem2d, w_mem_kv)


_XATTN_YIELDS = 2 * X_HEADS


def _xattn_block(q_ref, z_ref, k_ref, v_ref, out, col0):
    scale = X_HEAD_DIM ** -0.5
    for h in range(X_HEADS):
        hsl = slice(h * X_HEAD_DIM, (h + 1) * X_HEAD_DIM)
        s = lax.dot_general(q_ref[:, hsl], k_ref[:, hsl], _NT, preferred_element_type=F32) * scale
        m = jnp.max(s, axis=-1, keepdims=True)
        p = jnp.exp(s - m)
        denom = jnp.sum(p, axis=-1, keepdims=True)
        yield
        o = jnp.dot(p.astype(BF16), v_ref[:, hsl], preferred_element_type=F32) / denom
        out[:, col0 + h * X_HEAD_DIM:col0 + (h + 1) * X_HEAD_DIM] = (
            o * _silu(z_ref[:, hsl].astype(F32))).astype(out.dtype)
        yield


_LANES = 128


def _fused_kernel(sink_ref, relb_ref, bucket_ref,
                  cur_ref, kvp_ref, g_ref, cw_ref, cb_ref, gb_ref, ng_ref, mem_ref,
                  w_ref, x_ref, xc_ref, lng_ref, lnb_ref,
                  o_ref,
                  mix_cur, mix_new, rbuf, s1, s2, mean_s, rstd_s, tbl, xbuf, c_state, n_state, m_state,
                  *, n_tiles, blocks_per_seq):
    i, j = pl.program_id(0), pl.program_id(1)
    nj = pl.num_programs(1)
    tm, tn = x_ref.shape
    d_model = rbuf.shape[1]
    blk = jnp.minimum(i, n_tiles - 1) * nj + j
    col = pl.multiple_of(j * tn, tn)

    seg = lambda name: cur_ref.at[:, _proj_cols(name)]
    q_ref, kc_ref, vc_ref = seg("a_q"), seg("a_k"), seg("a_v")
    az = _proj_cols("a_z")
    z0_ref = cur_ref.at[:, az.start:az.start + SWA_W // 2]
    z1_ref = cur_ref.at[:, az.start + SWA_W // 2:az.stop]
    kp_ref, vp_ref = kvp_ref.at[:, 0:SWA_KV_W], kvp_ref.at[:, SWA_KV_W:2 * SWA_KV_W]
    mq_ref, mk_ref, mv_ref, mo_ref, mz_ref = seg("m_q"), seg("m_k"), seg("m_v"), seg("m_o"), seg("m_z")
    cq_ref, cz_ref = seg("c_q"), seg("c_z")
    memk_ref, memv_ref = mem_ref.at[:, 0:X_W], mem_ref.at[:, X_W:2 * X_W]

    @pl.when((i == 0) & (j == 0))
    def _():
        _swa_build_bias(bucket_ref, relb_ref, tbl)
        mix_cur[...] = jnp.zeros_like(mix_cur)
        mix_new[...] = jnp.zeros_like(mix_new)
        rbuf[...] = jnp.zeros_like(rbuf)
        s1[...] = jnp.zeros_like(s1)
        s2[...] = jnp.zeros_like(s2)
        mean_s[...] = jnp.zeros_like(mean_s)
        rstd_s[...] = jnp.zeros_like(rstd_s)

    @pl.when((i < n_tiles) & (blk % blocks_per_seq == 0))
    def _():
        _mlstm_reset(xbuf, c_state, n_state, m_state)

    def shift():
        return jnp.broadcast_to(DEEPNORM_ALPHA * xc_ref[:, 0:1], (tm, _LANES))

    def stage_c():
        mean, rstd = mean_s[...], rstd_s[...]
        for c in range(tn // _LANES):
            sl = slice(c * _LANES, (c + 1) * _LANES)
            dsl = pl.ds(col + c * _LANES, _LANES)
            o_ref[:, sl] = (rbuf[:, dsl] - mean) * rstd * lng_ref[:, dsl] + lnb_ref[:, dsl]

    def stage_b(kc, n_k):
        kw = d_model // n_k
        ksl = slice(kc * kw, (kc + 1) * kw)
        dsl = pl.ds(col, tn)
        part = jnp.dot(mix_cur[:, ksl], w_ref[ksl, :], preferred_element_type=F32)
        r = (DEEPNORM_ALPHA * x_ref[...] if kc == 0 else rbuf[:, dsl]) + part
        rbuf[:, dsl] = r
        if kc == n_k - 1:
            c0 = shift()
            a1, a2 = s1[...], s2[...]
            for l in range(tn // _LANES):
                d = r[:, l * _LANES:(l + 1) * _LANES] - c0
                a1 = a1 + d
                a2 = a2 + d * d
            s1[...] = a1
            s2[...] = a2

    def stage_a_pieces():
        out = mix_new.at[pl.ds(pl.multiple_of(j * BLOCK, BLOCK), BLOCK)]
        lane = lax.broadcasted_iota(jnp.int32, (1, 2 * BLOCK), 1)
        neg = jnp.where(blk % blocks_per_seq == 0, -jnp.inf, 0.0)
        first_mask = jnp.where(lane < BLOCK, neg, 0.0)
        yield from _swa_block(sink_ref, q_ref, z0_ref, z1_ref, kp_ref, kc_ref, vp_ref, vc_ref,
                              tbl, first_mask, out, range(SWA_KV_HEADS))
        yield from _mlstm_chunk(mq_ref, mk_ref, mv_ref, g_ref, mo_ref, mz_ref, cw_ref, cb_ref, gb_ref, ng_ref,
                                xbuf, c_state, n_state, m_state, out, SWA_W)
        yield from _xattn_block(cq_ref, cz_ref, memk_ref, memv_ref, out, SWA_W + M_V_W)

    n_a = _SWA_YIELDS_PER_KV * SWA_KV_HEADS + _MLSTM_YIELDS + _XATTN_YIELDS
    n_k = 16

    @pl.when(i < n_tiles)
    def _():
        stage_c()
        done_a = done_b = 0
        for _ in stage_a_pieces():
            done_a += 1
            while done_b < n_k and done_b * n_a < done_a * n_k:
                stage_b(done_b, n_k)
                done_b += 1
        assert done_a == n_a and done_b == n_k

    @pl.when(i == n_tiles)
    def _():
        stage_c()
        stage_b(0, 1)

    @pl.when(i == n_tiles + 1)
    def _():
        stage_c()

    @pl.when(j == nj - 1)
    def _():
        mean_d = jnp.sum(s1[...], axis=-1, keepdims=True) * (1.0 / d_model)
        var = jnp.sum(s2[...], axis=-1, keepdims=True) * (1.0 / d_model) - mean_d * mean_d
        mean_s[...] = shift() + jnp.broadcast_to(mean_d, (tm, _LANES))
        rstd_s[...] = jnp.broadcast_to(lax.rsqrt(var + LN_EPS), (tm, _LANES))
        s1[...] = jnp.zeros_like(s1)
        s2[...] = jnp.zeros_like(s2)
        mix_cur[...] = mix_new[...]


def _fused_mix_out(proj, gates, mkv, w_out_bf, x2d, rel_bias, sinks, conv_w, conv_b, gate_bias, norm_g,
                   ln_g, ln_b, *, batch, seq, tm=512, tn=1024):
    T, D = x2d.shape
    n_tiles = T // tm
    nj = D // tn
    blocks_per_seq = seq // BLOCK
    assert tm // BLOCK == nj and T % tm == 0 and D % tn == 0 and M_CHUNK == BLOCK and MIX_W == D

    def blk(i, j):
        return jnp.minimum(i, n_tiles - 1) * nj + j

    def cur(col):
        return lambda i, j, *_: (blk(i, j), col)

    def prev(col):
        return lambda i, j, *_: (jnp.maximum(blk(i, j) - 1, 0), col)

    def mem(col):
        return lambda i, j, *_: (blk(i, j) // blocks_per_seq, col)

    def const(i, j, *_):
        return (0, 0)

    def row_b(i):
        return jnp.clip(i - 1, 0, n_tiles - 1)

    kv = _proj_cols("a_k")
    assert _proj_cols("a_v").start == kv.stop and kv.start % (2 * SWA_KV_W) == 0
    in_specs = [
        pl.BlockSpec((1, 2 * BLOCK), const),
        pl.BlockSpec((BLOCK, PROJ_W), cur(0)),
        pl.BlockSpec((BLOCK, 2 * SWA_KV_W), prev(kv.start // (2 * SWA_KV_W))),
        pl.BlockSpec((BLOCK, GATE_W), cur(0)),
        pl.BlockSpec((CONV_WIDTH, 2 * M_QK_W), const),
        pl.BlockSpec((1, 2 * M_QK_W), const),
        pl.BlockSpec((1, GATE_W), const),
        pl.BlockSpec((1, M_V_W), const),
        pl.BlockSpec((MEM_LEN, 2 * X_W), mem(0), pipeline_mode=pl.Buffered(1)),
        pl.BlockSpec((D, tn), lambda i, j, *_: (0, j)),
        pl.BlockSpec((tm, tn), lambda i, j, *_: (row_b(i), j)),
        pl.BlockSpec((tm, _LANES), lambda i, j, *_: (row_b(i), 0)),
        pl.BlockSpec((1, D), const),
        pl.BlockSpec((1, D), const),
    ]
    grid_spec = pltpu.PrefetchScalarGridSpec(
        num_scalar_prefetch=2,
        grid=(n_tiles + 2, nj),
        in_specs=in_specs,
        out_specs=pl.BlockSpec((tm, tn), lambda i, j, *_: (jnp.maximum(i - 2, 0), jnp.where(i < 2, 0, j))),
        scratch_shapes=[
            pltpu.VMEM((tm, D), BF16),
            pltpu.VMEM((tm, D), BF16),
            pltpu.VMEM((tm, D), F32),
            pltpu.VMEM((tm, _LANES), F32),
            pltpu.VMEM((tm, _LANES), F32),
            pltpu.VMEM((tm, _LANES), F32),
            pltpu.VMEM((tm, _LANES), F32),
            pltpu.VMEM((SWA_HEADS, BLOCK, 2 * BLOCK), F32),
            pltpu.VMEM((M_CHUNK + _CONV_HALO, 2 * M_QK_W), F32),
            pltpu.VMEM((M_HEADS, M_QK_DIM, M_V_DIM), F32),
            pltpu.VMEM((8, M_QK_DIM), F32),
            pltpu.VMEM((8, M_CHUNK), F32),
        ],
    )
    return pl.pallas_call(
        functools.partial(_fused_kernel, n_tiles=n_tiles, blocks_per_seq=blocks_per_seq),
        out_shape=jax.ShapeDtypeStruct((T, D), F32),
        grid_spec=grid_spec,
        compiler_params=pltpu.CompilerParams(
            dimension_semantics=("arbitrary", "arbitrary"), vmem_limit_bytes=V7X_VMEM_LIMIT),
        name="mix_out",
    )(sinks, rel_bias.reshape(-1), jnp.asarray(_swa_bucket_row()),
      proj, proj, gates, conv_w, conv_b, gate_bias, norm_g, mkv,
      w_out_bf, x2d, x2d, ln_g, ln_b)


def kernel(x, mem, w_in, conv_w, conv_b, b_i, b_f, m_norm_g, rel_bias, sinks, w_mem_kv, w_out, ln_g, ln_b):
    B, S, D = x.shape
    assert D == D_MODEL and S % BLOCK == 0 and S % M_CHUNK == 0
    T = B * S
    x2d = x.reshape(T, D)
    gate_bias = jnp.concatenate(
        [b_i.astype(F32), b_f.astype(F32), jnp.zeros((GATE_W - GATE_COLS,), F32)])[None, :]

    proj, gates = _in_proj(x2d.astype(BF16), w_in.T)
    mkv = _mem_kv(mem.reshape(B * MEM_LEN, D), w_mem_kv)
    out = _fused_mix_out(proj, gates, mkv, w_out.astype(BF16), x2d, rel_bias.astype(F32), sinks.astype(F32),
                         conv_w.astype(F32), conv_b.astype(F32)[None, :], gate_bias,
                         m_norm_g.astype(F32)[None, :], ln_g.astype(F32)[None, :], ln_b.astype(F32)[None, :],
                         batch=B, seq=S)
    return out.reshape(B, S, D).astype(x.dtype)
```

```python
import functools
import math

import jax
import jax.numpy as jnp
import numpy as np
from jax import lax
from jax.experimental import pallas as pl
from jax.experimental.pallas import tpu as pltpu

F32 = jnp.float32
BF16 = jnp.bfloat16

D_MODEL = 4096
MEM_LEN = 256
SWA_HEADS = 16
SWA_KV_HEADS = 4
SWA_GROUP = SWA_HEADS // SWA_KV_HEADS
SWA_HEAD_DIM = 128
WINDOW = 128
BLOCK = 128
M_HEADS = 4
M_QK_DIM = 128
M_V_DIM = 256
M_CHUNK = 128
CONV_WIDTH = 4
X_HEADS = 4
X_HEAD_DIM = 256
NUM_BUCKETS = 32
MAX_DISTANCE = 128

SWA_W = SWA_HEADS * SWA_HEAD_DIM
SWA_KV_W = SWA_KV_HEADS * SWA_HEAD_DIM
M_QK_W = M_HEADS * M_QK_DIM
M_V_W = M_HEADS * M_V_DIM
X_W = X_HEADS * X_HEAD_DIM
MIX_W = SWA_W + M_V_W + X_W

DEEPNORM_ALPHA = 2.0 ** 0.25
LN_EPS = 1e-5
HEAD_NORM_EPS = 1e-6

_REF_SPLITS = (
    ("a_q", SWA_W), ("a_k", SWA_KV_W), ("a_v", SWA_KV_W), ("a_z", SWA_W),
    ("m_q", M_QK_W), ("m_k", M_QK_W), ("m_v", M_V_W), ("m_i", M_HEADS), ("m_f", M_HEADS),
    ("m_o", M_V_W), ("m_z", M_V_W), ("c_q", X_W), ("c_z", X_W),
)
GATE_W = 128
V7X_VMEM_BYTES = 64 * 1024 * 1024
V7X_VMEM_LIMIT = V7X_VMEM_BYTES - 4 * 1024 * 1024


def _ref_offsets():
    offs, acc = {}, 0
    for name, width in _REF_SPLITS:
        offs[name] = acc
        acc += width
    return offs


_REF_OFFS = _ref_offsets()
HEAD_W = _REF_OFFS["m_i"]
GATE_COLS = 2 * M_HEADS
TAIL_START = HEAD_W + GATE_COLS
TAIL_W = 2 * M_V_W + 2 * X_W
PROJ_W = HEAD_W + TAIL_W
_HEAD_SEGS = ("a_q", "a_k", "a_v", "a_z", "m_q", "m_k", "m_v")


def _proj_cols(name):
    off = _REF_OFFS[name] - (0 if name in _HEAD_SEGS else GATE_COLS)
    return slice(off, off + dict(_REF_SPLITS)[name])


def _silu(z):
    return z * (1.0 / (1.0 + jnp.exp(-z)))


def _sigmoid(z):
    return 1.0 / (1.0 + jnp.exp(-z))


def _log_sigmoid(z):
    return jnp.minimum(z, 0.0) - jnp.log1p(jnp.exp(-jnp.abs(z)))


_NT = (((1,), (1,)), ((), ()))


def _in_proj_kernel(xa_ref, xb_ref, w_ref, wg_ref, wo_ref, o_ref, g_ref, wo_bf_ref, *, n_cast_steps):
    half = xa_ref.shape[0]

    @pl.when(pl.program_id(0) * pl.num_programs(1) + pl.program_id(1) < n_cast_steps)
    def _():
        wo_bf_ref[...] = wo_ref[...].astype(BF16)

    w = w_ref[...].astype(BF16)
    for r, x_ref in enumerate((xa_ref, xb_ref)):
        rows = slice(r * half, (r + 1) * half)
        o_ref[rows, :] = lax.dot_general(x_ref[...], w, _NT, preferred_element_type=F32).astype(o_ref.dtype)

    @pl.when(pl.program_id(1) == 0)
    def _():
        wg = wg_ref[...].astype(BF16)
        for r, x_ref in enumerate((xa_ref, xb_ref)):
            rows = slice(r * half, (r + 1) * half)
            g_ref[rows, :] = lax.dot_general(x_ref[...], wg, _NT, preferred_element_type=F32)


def _in_proj(x_bf, w_t, w_out, *, tm=2048, tn=512, cast_rows=64):
    T, K = x_bf.shape
    n_head, n_tail = HEAD_W // tn, TAIL_W // tn
    nj = n_head + n_tail
    assert HEAD_W % tn == 0 and TAIL_W % tn == 0 and HEAD_W % GATE_W == 0
    n_cast_steps = w_out.shape[0] // cast_rows
    assert w_out.shape[0] % cast_rows == 0 and n_cast_steps <= (T // tm) * nj

    def cast_blk(i, j):
        return jnp.minimum(i * nj + j, n_cast_steps - 1), 0

    def w_row(i, j):
        row = jnp.where(j < n_head, j * tn, TAIL_START + (j - n_head) * tn)
        return pl.multiple_of(row, math.gcd(tn, TAIL_START)), 0

    return pl.pallas_call(
        functools.partial(_in_proj_kernel, n_cast_steps=n_cast_steps),
        out_shape=(jax.ShapeDtypeStruct((T, PROJ_W), BF16), jax.ShapeDtypeStruct((T, GATE_W), F32),
                   jax.ShapeDtypeStruct(w_out.shape, BF16)),
        grid=(T // tm, nj),
        in_specs=[
            pl.BlockSpec((tm // 2, K), lambda i, j: (2 * i, 0)),
            pl.BlockSpec((tm // 2, K), lambda i, j: (2 * i + 1, 0), pipeline_mode=pl.Buffered(1)),
            pl.BlockSpec((pl.Element(tn), pl.Element(K)), w_row),
            pl.BlockSpec((GATE_W, K), lambda i, j: (HEAD_W // GATE_W, 0)),
            pl.BlockSpec((cast_rows, w_out.shape[1]), cast_blk),
        ],
        out_specs=(
            pl.BlockSpec((tm, tn), lambda i, j: (i, j)),
            pl.BlockSpec((tm, GATE_W), lambda i, j: (i, 0)),
            pl.BlockSpec((cast_rows, w_out.shape[1]), cast_blk),
        ),
        compiler_params=pltpu.CompilerParams(
            dimension_semantics=("arbitrary", "arbitrary"),
            vmem_limit_bytes=V7X_VMEM_LIMIT),
        name="in_proj",
    )(x_bf, x_bf, w_t, w_t, w_out)


def _t5_bucket_np(dist):
    max_exact = NUM_BUCKETS // 2
    is_small = dist < max_exact
    ratio = np.maximum(dist, max_exact).astype(np.float32) / np.float32(max_exact)
    large = max_exact + (np.log(ratio) / np.float32(math.log(MAX_DISTANCE / max_exact))
                         * np.float32(NUM_BUCKETS - max_exact)).astype(np.int32)
    large = np.minimum(large, NUM_BUCKETS - 1)
    return np.where(is_small, dist, large)


def _swa_bucket_row():
    c = np.arange(2 * BLOCK, dtype=np.int32)
    return _t5_bucket_np(np.clip(BLOCK - c, 0, MAX_DISTANCE - 1)).astype(np.int32)[None, :]


def _swa_build_bias(bucket_ref, relb_ref, tbl_ref):
    q = lax.broadcasted_iota(jnp.int32, (BLOCK, 2 * BLOCK), 0)
    s = lax.broadcasted_iota(jnp.int32, (BLOCK, 2 * BLOCK), 1)
    dist = q + BLOCK - s
    in_win = jnp.where(dist >= 0, jnp.where(dist < WINDOW, 1, 0), 0)
    bucket = bucket_ref[...]
    for h in range(SWA_HEADS):
        u = jnp.zeros((1, 2 * BLOCK), F32)
        for bkt in range(NUM_BUCKETS):
            u = jnp.where(bucket == bkt, relb_ref[bkt * SWA_HEADS + h], u)
        t = pltpu.roll(jnp.broadcast_to(u, (BLOCK, 2 * BLOCK)), 0, 1, stride=1, stride_axis=0)
        tbl_ref[h] = jnp.where(in_win == 1, t, -jnp.inf)


_SWA_YIELDS_PER_KV = 2


def _swa_block(sink_ref, q_ref, z0_ref, z1_ref, kp_ref, kc_ref, vp_ref, vc_ref, tbl_ref, first_mask, out,
               kv_heads):
    scale = SWA_HEAD_DIM ** -0.5
    half = SWA_HEADS // 2
    for kv in kv_heads:
        ksl = slice(kv * SWA_HEAD_DIM, (kv + 1) * SWA_HEAD_DIM)
        kk = jnp.concatenate([kp_ref[:, ksl], kc_ref[:, ksl]], axis=0)
        vv = jnp.concatenate([vp_ref[:, ksl], vc_ref[:, ksl]], axis=0)
        heads = [kv * SWA_GROUP + g for g in range(SWA_GROUP)]
        q4 = jnp.concatenate(
            [q_ref[:, h * SWA_HEAD_DIM:(h + 1) * SWA_HEAD_DIM] for h in heads], axis=0)
        s4 = lax.dot_general(q4, kk, _NT, preferred_element_type=F32)
        ps, denoms = [], []
        for g, h in enumerate(heads):
            s = s4[g * BLOCK:(g + 1) * BLOCK] * scale + (tbl_ref[h] + first_mask)
            sink = sink_ref[h]
            m = jnp.maximum(jnp.max(s, axis=-1, keepdims=True), sink)
            p = jnp.exp(s - m)
            denoms.append(jnp.sum(p, axis=-1, keepdims=True) + jnp.exp(sink - m))
            ps.append(p.astype(BF16))
        yield
        o4 = jnp.dot(jnp.concatenate(ps, axis=0), vv, preferred_element_type=F32)
        for g, h in enumerate(heads):
            z_ref = z0_ref if h < half else z1_ref
            zsl = slice((h % half) * SWA_HEAD_DIM, (h % half + 1) * SWA_HEAD_DIM)
            o = o4[g * BLOCK:(g + 1) * BLOCK] / denoms[g]
            out[:, h * SWA_HEAD_DIM:(h + 1) * SWA_HEAD_DIM] = (
                o * _silu(z_ref[:, zsl].astype(F32))).astype(out.dtype)
        yield


def _cumsum_lanes(x):
    n = x.shape[-1]
    lane = lax.broadcasted_iota(jnp.int32, x.shape, x.ndim - 1)
    shift = 1
    while shift < n:
        x = x + jnp.where(lane >= shift, pltpu.roll(x, shift, x.ndim - 1), 0.0)
        shift *= 2
    return x


_CONV_HALO = 8


def _mlstm_reset(xbuf, c_state, n_state, m_state):
    xbuf[0:_CONV_HALO, :] = jnp.zeros((_CONV_HALO, 2 * M_QK_W), F32)
    c_state[...] = jnp.zeros_like(c_state)
    n_state[...] = jnp.zeros_like(n_state)
    m_state[...] = jnp.zeros_like(m_state)


_MLSTM_YIELDS = 1 + 3 * M_HEADS


def _mlstm_chunk(q_ref, k_ref, v_ref, g_ref, o_ref, z_ref, cw_ref, cb_ref, gb_ref, ng_ref,
                 xbuf, c_state, n_state, m_state, out, col0):
    L = M_CHUNK
    HALO = _CONV_HALO

    xbuf[HALO:HALO + L, 0:M_QK_W] = q_ref[...].astype(F32)
    xbuf[HALO:HALO + L, M_QK_W:] = k_ref[...].astype(F32)
    acc = cb_ref[...] + cw_ref[CONV_WIDTH - 1:CONV_WIDTH, :] * xbuf[HALO:HALO + L, :]
    for j in range(CONV_WIDTH - 1):
        back = CONV_WIDTH - 1 - j
        acc = acc + cw_ref[j:j + 1, :] * xbuf[HALO - back:HALO - back + L, :]
    tail = xbuf[L:L + HALO, :]
    xbuf[0:HALO, :] = tail
    qk = _silu(acc)
    q_all = qk[:, 0:M_QK_W]
    k_all = qk[:, M_QK_W:] * (M_QK_DIM ** -0.5)

    gt = (g_ref[...] + gb_ref[...]).T[0:8, :]
    row8 = lax.broadcasted_iota(jnp.int32, (8, L), 0)
    head_rows = row8 >= M_HEADS
    log_i = jnp.where(head_rows, pltpu.roll(gt, M_HEADS, 0), 0.0)
    log_f = jnp.where(head_rows, _log_sigmoid(gt), 0.0)
    b = _cumsum_lanes(log_f)
    g_tot = b[:, L - 1:L]
    a = g_tot - b + log_i
    m_loc = jnp.max(a, axis=-1, keepdims=True)
    w = jnp.exp(a - m_loc)
    m_prev = m_state[...]
    inter = b + m_prev
    r = log_i - b

    cols = jnp.concatenate([inter, w, b, jnp.zeros((L - 24, L), F32)], axis=0).T

    g_rep = jnp.broadcast_to(g_tot, (8, L))
    m_loc_rep = jnp.broadcast_to(m_loc, (8, L))
    m_new = jnp.maximum(g_rep + m_prev, m_loc_rep)
    s_prev = jnp.exp(g_rep + m_prev - m_new)
    s_loc = jnp.exp(m_loc_rep - m_new)
    n_prev_all = n_state[...]

    ti = lax.broadcasted_iota(jnp.int32, (L, L), 0)
    si = lax.broadcasted_iota(jnp.int32, (L, L), 1)
    causal = ti >= si
    yield

    for h in range(M_HEADS):
        rr = M_HEADS + h
        qh = q_all[:, h * M_QK_DIM:(h + 1) * M_QK_DIM]
        kh = k_all[:, h * M_QK_DIM:(h + 1) * M_QK_DIM]
        vsl = slice(h * M_V_DIM, (h + 1) * M_V_DIM)
        vh = v_ref[:, vsl]
        inter_c = cols[:, rr:rr + 1]
        w_c = cols[:, 8 + rr:9 + rr]
        b_c = cols[:, 16 + rr:17 + rr]

        dmat = jnp.where(causal, b_c + r[rr:rr + 1, :], -jnp.inf)
        m_t = jnp.maximum(inter_c, jnp.max(dmat, axis=-1, keepdims=True))
        qb = qh.astype(BF16)
        kb = kh.astype(BF16)
        s = lax.dot_general(qb, kb, _NT, preferred_element_type=F32)
        sm = s * jnp.exp(dmat - m_t)
        w_inter = jnp.exp(inter_c - m_t)
        yield
        c_prev = c_state[h]
        num = (jnp.dot(sm.astype(BF16), vh, preferred_element_type=F32)
               + w_inter * jnp.dot(qb, c_prev.astype(BF16), preferred_element_type=F32))
        n_prev = n_prev_all[rr:rr + 1, :]
        den = (jnp.sum(sm, axis=-1, keepdims=True)
               + w_inter * jnp.sum(qh * n_prev, axis=-1, keepdims=True))
        hcell = num / jnp.maximum(jnp.abs(den), jnp.exp(-m_t))

        hg = _sigmoid(o_ref[:, vsl].astype(F32)) * hcell
        mu = jnp.mean(hg, axis=-1, keepdims=True)
        var = jnp.mean(jnp.square(hg - mu), axis=-1, keepdims=True)
        hn = (hg - mu) * lax.rsqrt(var + HEAD_NORM_EPS) * ng_ref[:, vsl]
        out[:, col0 + h * M_V_DIM:col0 + (h + 1) * M_V_DIM] = (
            hn * _silu(z_ref[:, vsl].astype(F32))).astype(out.dtype)
        yield

        wk = w_c * kh
        c_loc = jnp.dot(wk.T.astype(BF16), vh, preferred_element_type=F32)
        sp_row = s_prev[rr:rr + 1, :]
        sl_row = s_loc[rr:rr + 1, :]
        sp2 = jnp.concatenate([sp_row, sp_row], axis=1)
        sl2 = jnp.concatenate([sl_row, sl_row], axis=1)
        c_state[h] = sp2 * c_prev + sl2 * c_loc
        n_loc = jnp.sum(wk, axis=0, keepdims=True)
        n_state[rr:rr + 1, :] = sp_row * n_prev + sl_row * n_loc
        yield

    m_state[...] = m_new


def _mem_kv_kernel(mem_ref, w_ref, o_ref):
    o_ref[...] = jnp.dot(mem_ref[...].astype(BF16), w_ref[...].astype(BF16),
                         preferred_element_type=F32).astype(o_ref.dtype)


def _mem_kv(mem2d, w_mem_kv, *, tn=512):
    M, K = mem2d.shape
    N = w_mem_kv.shape[1]
    return pl.pallas_call(
        _mem_kv_kernel,
        out_shape=jax.ShapeDtypeStruct((M, N), BF16),
        grid=(N // tn,),
        in_specs=[pl.BlockSpec((M, K), lambda j: (0, 0)),
                  pl.BlockSpec((K, tn), lambda j: (0, j))],
        out_specs=pl.BlockSpec((M, tn), lambda j: (0, j)),
        compiler_params=pltpu.CompilerParams(
            dimension_semantics=("arbitrary",), vmem_limit_bytes=V7X_VMEM_LIMIT),
        name="mem_kv",
    )(mem2d, w_mem_kv)


_XATTN_YIELDS = 2 * X_HEADS


def _xattn_block(q_ref, z_ref, k_ref, v_ref, out, col0):
    scale = X_HEAD_DIM ** -0.5
    for h in range(X_HEADS):
        hsl = slice(h * X_HEAD_DIM, (h + 1) * X_HEAD_DIM)
        s = lax.dot_general(q_ref[:, hsl], k_ref[:, hsl], _NT, preferred_element_type=F32) * scale
        m = jnp.max(s, axis=-1, keepdims=True)
        p = jnp.exp(s - m)
        denom = jnp.sum(p, axis=-1, keepdims=True)
        yield
        o = jnp.dot(p.astype(BF16), v_ref[:, hsl], preferred_element_type=F32) / denom
        out[:, col0 + h * X_HEAD_DIM:col0 + (h + 1) * X_HEAD_DIM] = (
            o * _silu(z_ref[:, hsl].astype(F32))).astype(out.dtype)
        yield


_LANES = 128


def _fused_kernel(sink_ref, relb_ref, bucket_ref,
                  cur_ref, kvp_ref, g_ref, cw_ref, cb_ref, gb_ref, ng_ref, mem_ref,
                  w_ref, x_ref, xc_ref, lng_ref, lnb_ref,
                  o_ref,
                  mix_cur, mix_new, rbuf, s1, s2, mean_s, rstd_s, tbl, xbuf, c_state, n_state, m_state,
                  *, n_tiles, blocks_per_seq):
    i, j = pl.program_id(0), pl.program_id(1)
    nj = pl.num_programs(1)
    tm, tn = x_ref.shape
    d_model = rbuf.shape[1]
    blk = jnp.minimum(i, n_tiles - 1) * nj + j
    col = pl.multiple_of(j * tn, tn)

    seg = lambda name: cur_ref.at[:, _proj_cols(name)]
    q_ref, kc_ref, vc_ref = seg("a_q"), seg("a_k"), seg("a_v")
    az = _proj_cols("a_z")
    z0_ref = cur_ref.at[:, az.start:az.start + SWA_W // 2]
    z1_ref = cur_ref.at[:, az.start + SWA_W // 2:az.stop]
    kp_ref, vp_ref = kvp_ref.at[:, 0:SWA_KV_W], kvp_ref.at[:, SWA_KV_W:2 * SWA_KV_W]
    mq_ref, mk_ref, mv_ref, mo_ref, mz_ref = seg("m_q"), seg("m_k"), seg("m_v"), seg("m_o"), seg("m_z")
    cq_ref, cz_ref = seg("c_q"), seg("c_z")
    memk_ref, memv_ref = mem_ref.at[:, 0:X_W], mem_ref.at[:, X_W:2 * X_W]

    @pl.when((i == 0) & (j == 0))
    def _():
        _swa_build_bias(bucket_ref, relb_ref, tbl)
        mix_cur[...] = jnp.zeros_like(mix_cur)
        mix_new[...] = jnp.zeros_like(mix_new)
        rbuf[...] = jnp.zeros_like(rbuf)
        s1[...] = jnp.zeros_like(s1)
        s2[...] = jnp.zeros_like(s2)
        mean_s[...] = jnp.zeros_like(mean_s)
        rstd_s[...] = jnp.zeros_like(rstd_s)

    @pl.when((i < n_tiles) & (blk % blocks_per_seq == 0))
    def _():
        _mlstm_reset(xbuf, c_state, n_state, m_state)

    def shift():
        return jnp.broadcast_to(DEEPNORM_ALPHA * xc_ref[:, 0:1], (tm, _LANES))

    def stage_c():
        mean, rstd = mean_s[...], rstd_s[...]
        for c in range(tn // _LANES):
            sl = slice(c * _LANES, (c + 1) * _LANES)
            dsl = pl.ds(col + c * _LANES, _LANES)
            o_ref[:, sl] = (rbuf[:, dsl] - mean) * rstd * lng_ref[:, dsl] + lnb_ref[:, dsl]

    def stage_b(c, n_k, n_n):
        kc, nc = divmod(c, n_n)
        kw, nw = d_model // n_k, tn // n_n
        ksl = slice(kc * kw, (kc + 1) * kw)
        nsl = slice(nc * nw, (nc + 1) * nw)
        dsl = pl.ds(col + nc * nw, nw)
        part = jnp.dot(mix_cur[:, ksl], w_ref[ksl, nsl], preferred_element_type=F32)
        r = (DEEPNORM_ALPHA * x_ref[:, nsl] if kc == 0 else rbuf[:, dsl]) + part
        rbuf[:, dsl] = r
        if kc == n_k - 1:
            c0 = shift()
            a1, a2 = s1[...], s2[...]
            for l in range(nw // _LANES):
                d = r[:, l * _LANES:(l + 1) * _LANES] - c0
                a1 = a1 + d
                a2 = a2 + d * d
            s1[...] = a1
            s2[...] = a2

    def stage_a_pieces():
        out = mix_new.at[pl.ds(pl.multiple_of(j * BLOCK, BLOCK), BLOCK)]
        lane = lax.broadcasted_iota(jnp.int32, (1, 2 * BLOCK), 1)
        neg = jnp.where(blk % blocks_per_seq == 0, -jnp.inf, 0.0)
        first_mask = jnp.where(lane < BLOCK, neg, 0.0)
        yield from _swa_block(sink_ref, q_ref, z0_ref, z1_ref, kp_ref, kc_ref, vp_ref, vc_ref,
                              tbl, first_mask, out, range(SWA_KV_HEADS))
        yield from _mlstm_chunk(mq_ref, mk_ref, mv_ref, g_ref, mo_ref, mz_ref, cw_ref, cb_ref, gb_ref, ng_ref,
                                xbuf, c_state, n_state, m_state, out, SWA_W)
        yield from _xattn_block(cq_ref, cz_ref, memk_ref, memv_ref, out, SWA_W + M_V_W)

    n_a = _SWA_YIELDS_PER_KV * SWA_KV_HEADS + _MLSTM_YIELDS + _XATTN_YIELDS
    n_k, n_n = 16, 1
    n_b = n_k * n_n

    @pl.when(i < n_tiles)
    def _():
        stage_c()
        done_a = done_b = 0
        for _ in stage_a_pieces():
            done_a += 1
            while done_b < n_b and done_b * n_a < done_a * n_b:
                stage_b(done_b, n_k, n_n)
                done_b += 1
        assert done_a == n_a and done_b == n_b

    @pl.when(i == n_tiles)
    def _():
        stage_c()
        stage_b(0, 1, 1)

    @pl.when(i == n_tiles + 1)
    def _():
        stage_c()

    @pl.when(j == nj - 1)
    def _():
        mean_d = jnp.sum(s1[...], axis=-1, keepdims=True) * (1.0 / d_model)
        var = jnp.sum(s2[...], axis=-1, keepdims=True) * (1.0 / d_model) - mean_d * mean_d
        mean_s[...] = shift() + jnp.broadcast_to(mean_d, (tm, _LANES))
        rstd_s[...] = jnp.broadcast_to(lax.rsqrt(var + LN_EPS), (tm, _LANES))
        s1[...] = jnp.zeros_like(s1)
        s2[...] = jnp.zeros_like(s2)
        mix_cur[...] = mix_new[...]


def _fused_mix_out(proj, gates, mkv, w_out_bf, x2d, rel_bias, sinks, conv_w, conv_b, gate_bias, norm_g,
                   ln_g, ln_b, *, batch, seq, tm=512, tn=1024):
    T, D = x2d.shape
    n_tiles = T // tm
    nj = D // tn
    blocks_per_seq = seq // BLOCK
    assert tm // BLOCK == nj and T % tm == 0 and D % tn == 0 and M_CHUNK == BLOCK and MIX_W == D

    def blk(i, j):
        return jnp.minimum(i, n_tiles - 1) * nj + j

    def cur(col):
        return lambda i, j, *_: (blk(i, j), col)

    def prev(col):
        return lambda i, j, *_: (jnp.maximum(blk(i, j) - 1, 0), col)

    def mem(col):
        return lambda i, j, *_: (blk(i, j) // blocks_per_seq, col)

    def const(i, j, *_):
        return (0, 0)

    def row_b(i):
        return jnp.clip(i - 1, 0, n_tiles - 1)

    kv = _proj_cols("a_k")
    assert _proj_cols("a_v").start == kv.stop and kv.start % (2 * SWA_KV_W) == 0
    in_specs = [
        pl.BlockSpec((1, 2 * BLOCK), const),
        pl.BlockSpec((BLOCK, PROJ_W), cur(0)),
        pl.BlockSpec((BLOCK, 2 * SWA_KV_W), prev(kv.start // (2 * SWA_KV_W))),
        pl.BlockSpec((BLOCK, GATE_W), cur(0)),
        pl.BlockSpec((CONV_WIDTH, 2 * M_QK_W), const),
        pl.BlockSpec((1, 2 * M_QK_W), const),
        pl.BlockSpec((1, GATE_W), const),
        pl.BlockSpec((1, M_V_W), const),
        pl.BlockSpec((MEM_LEN, 2 * X_W), mem(0), pipeline_mode=pl.Buffered(1)),
        pl.BlockSpec((D, tn), lambda i, j, *_: (0, j)),
        pl.BlockSpec((tm, tn), lambda i, j, *_: (row_b(i), j)),
        pl.BlockSpec((tm, _LANES), lambda i, j, *_: (row_b(i), 0)),
        pl.BlockSpec((1, D), const),
        pl.BlockSpec((1, D), const),
    ]
    grid_spec = pltpu.PrefetchScalarGridSpec(
        num_scalar_prefetch=2,
        grid=(n_tiles + 2, nj),
        in_specs=in_specs,
        out_specs=pl.BlockSpec((tm, tn), lambda i, j, *_: (jnp.maximum(i - 2, 0), jnp.where(i < 2, 0, j))),
        scratch_shapes=[
            pltpu.VMEM((tm, D), BF16),
            pltpu.VMEM((tm, D), BF16),
            pltpu.VMEM((tm, D), F32),
            pltpu.VMEM((tm, _LANES), F32),
            pltpu.VMEM((tm, _LANES), F32),
            pltpu.VMEM((tm, _LANES), F32),
            pltpu.VMEM((tm, _LANES), F32),
            pltpu.VMEM((SWA_HEADS, BLOCK, 2 * BLOCK), F32),
            pltpu.VMEM((M_CHUNK + _CONV_HALO, 2 * M_QK_W), F32),
            pltpu.VMEM((M_HEADS, M_QK_DIM, M_V_DIM), F32),
            pltpu.VMEM((8, M_QK_DIM), F32),
            pltpu.VMEM((8, M_CHUNK), F32),
        ],
    )
    return pl.pallas_call(
        functools.partial(_fused_kernel, n_tiles=n_tiles, blocks_per_seq=blocks_per_seq),
        out_shape=jax.ShapeDtypeStruct((T, D), F32),
        grid_spec=grid_spec,
        compiler_params=pltpu.CompilerParams(
            dimension_semantics=("arbitrary", "arbitrary"), vmem_limit_bytes=V7X_VMEM_LIMIT),
        name="mix_out",
    )(sinks, rel_bias.reshape(-1), jnp.asarray(_swa_bucket_row()),
      proj, proj, gates, conv_w, conv_b, gate_bias, norm_g, mkv,
      w_out_bf, x2d, x2d, ln_g, ln_b)


def kernel(x, mem, w_in, conv_w, conv_b, b_i, b_f, m_norm_g, rel_bias, sinks, w_mem_kv, w_out, ln_g, ln_b):
    B, S, D = x.shape
    assert D == D_MODEL and S % BLOCK == 0 and S % M_CHUNK == 0
    T = B * S
    x2d = x.reshape(T, D)
    gate_bias = jnp.concatenate(
        [b_i.astype(F32), b_f.astype(F32), jnp.zeros((GATE_W - GATE_COLS,), F32)])[None, :]

    proj, gates, w_out_bf = _in_proj(x2d.astype(BF16), w_in.T, w_out)
    mkv = _mem_kv(mem.reshape(B * MEM_LEN, D), w_mem_kv)
    out = _fused_mix_out(proj, gates, mkv, w_out_bf, x2d, rel_bias.astype(F32), sinks.astype(F32),
                         conv_w.astype(F32), conv_b.astype(F32)[None, :], gate_bias,
                         m_norm_g.astype(F32)[None, :], ln_g.astype(F32)[None, :], ln_b.astype(F32)[None, :],
                         batch=B, seq=S)
    return out.reshape(B, S, D).astype(x.dtype)
```

```python
import functools
import math

import jax
import jax.numpy as jnp
import numpy as np
from jax import lax
from jax.experimental import pallas as pl
from jax.experimental.pallas import tpu as pltpu

F32 = jnp.float32
BF16 = jnp.bfloat16

D_MODEL = 4096
MEM_LEN = 256
SWA_HEADS = 16
SWA_KV_HEADS = 4
SWA_GROUP = SWA_HEADS // SWA_KV_HEADS
SWA_HEAD_DIM = 128
WINDOW = 128
BLOCK = 128
M_HEADS = 4
M_QK_DIM = 128
M_V_DIM = 256
M_CHUNK = 128
CONV_WIDTH = 4
X_HEADS = 4
X_HEAD_DIM = 256
NUM_BUCKETS = 32
MAX_DISTANCE = 128

SWA_W = SWA_HEADS * SWA_HEAD_DIM
SWA_KV_W = SWA_KV_HEADS * SWA_HEAD_DIM
M_QK_W = M_HEADS * M_QK_DIM
M_V_W = M_HEADS * M_V_DIM
X_W = X_HEADS * X_HEAD_DIM
MIX_W = SWA_W + M_V_W + X_W

DEEPNORM_ALPHA = 2.0 ** 0.25
LN_EPS = 1e-5
HEAD_NORM_EPS = 1e-6

_REF_SPLITS = (
    ("a_q", SWA_W), ("a_k", SWA_KV_W), ("a_v", SWA_KV_W), ("a_z", SWA_W),
    ("m_q", M_QK_W), ("m_k", M_QK_W), ("m_v", M_V_W), ("m_i", M_HEADS), ("m_f", M_HEADS),
    ("m_o", M_V_W), ("m_z", M_V_W), ("c_q", X_W), ("c_z", X_W),
)
GATE_W = 128
V7X_VMEM_BYTES = 64 * 1024 * 1024
V7X_VMEM_LIMIT = V7X_VMEM_BYTES - 4 * 1024 * 1024


def _ref_offsets():
    offs, acc = {}, 0
    for name, width in _REF_SPLITS:
        offs[name] = acc
        acc += width
    return offs


_REF_OFFS = _ref_offsets()
HEAD_W = _REF_OFFS["m_i"]
GATE_COLS = 2 * M_HEADS
TAIL_START = HEAD_W + GATE_COLS
TAIL_W = 2 * M_V_W + 2 * X_W
PROJ_W = HEAD_W + TAIL_W
_HEAD_SEGS = ("a_q", "a_k", "a_v", "a_z", "m_q", "m_k", "m_v")


def _proj_cols(name):
    off = _REF_OFFS[name] - (0 if name in _HEAD_SEGS else GATE_COLS)
    return slice(off, off + dict(_REF_SPLITS)[name])


def _silu(z):
    return z * (1.0 / (1.0 + jnp.exp(-z)))


def _sigmoid(z):
    return 1.0 / (1.0 + jnp.exp(-z))


def _log_sigmoid(z):
    return jnp.minimum(z, 0.0) - jnp.log1p(jnp.exp(-jnp.abs(z)))


_NT = (((1,), (1,)), ((), ()))


def _in_proj_kernel(xa_ref, xb_ref, w_ref, wg_ref, wo_ref, o_ref, g_ref, wo_bf_ref, *, n_cast_steps):
    half = xa_ref.shape[0]

    @pl.when(pl.program_id(0) * pl.num_programs(1) + pl.program_id(1) < n_cast_steps)
    def _():
        wo_bf_ref[...] = wo_ref[...].astype(BF16)

    w = w_ref[...].astype(BF16)
    for r, x_ref in enumerate((xa_ref, xb_ref)):
        rows = slice(r * half, (r + 1) * half)
        o_ref[rows, :] = lax.dot_general(x_ref[...], w, _NT, preferred_element_type=F32).astype(o_ref.dtype)

    @pl.when(pl.program_id(1) == 0)
    def _():
        wg = wg_ref[...].astype(BF16)
        for r, x_ref in enumerate((xa_ref, xb_ref)):
            rows = slice(r * half, (r + 1) * half)
            g_ref[rows, :] = lax.dot_general(x_ref[...], wg, _NT, preferred_element_type=F32)


def _in_proj(x_bf, w_t, w_out, *, tm=2048, tn=512, cast_rows=64):
    T, K = x_bf.shape
    n_head, n_tail = HEAD_W // tn, TAIL_W // tn
    nj = n_head + n_tail
    assert HEAD_W % tn == 0 and TAIL_W % tn == 0 and HEAD_W % GATE_W == 0
    n_cast_steps = w_out.shape[0] // cast_rows
    assert w_out.shape[0] % cast_rows == 0 and n_cast_steps <= (T // tm) * nj

    def cast_blk(i, j):
        return jnp.minimum(i * nj + j, n_cast_steps - 1), 0

    def w_row(i, j):
        row = jnp.where(j < n_head, j * tn, TAIL_START + (j - n_head) * tn)
        return pl.multiple_of(row, math.gcd(tn, TAIL_START)), 0

    return pl.pallas_call(
        functools.partial(_in_proj_kernel, n_cast_steps=n_cast_steps),
        out_shape=(jax.ShapeDtypeStruct((T, PROJ_W), BF16), jax.ShapeDtypeStruct((T, GATE_W), F32),
                   jax.ShapeDtypeStruct(w_out.shape, BF16)),
        grid=(T // tm, nj),
        in_specs=[
            pl.BlockSpec((tm // 2, K), lambda i, j: (2 * i, 0)),
            pl.BlockSpec((tm // 2, K), lambda i, j: (2 * i + 1, 0), pipeline_mode=pl.Buffered(1)),
            pl.BlockSpec((pl.Element(tn), pl.Element(K)), w_row),
            pl.BlockSpec((GATE_W, K), lambda i, j: (HEAD_W // GATE_W, 0)),
            pl.BlockSpec((cast_rows, w_out.shape[1]), cast_blk),
        ],
        out_specs=(
            pl.BlockSpec((tm, tn), lambda i, j: (i, j)),
            pl.BlockSpec((tm, GATE_W), lambda i, j: (i, 0)),
            pl.BlockSpec((cast_rows, w_out.shape[1]), cast_blk),
        ),
        compiler_params=pltpu.CompilerParams(
            dimension_semantics=("arbitrary", "arbitrary"),
            vmem_limit_bytes=V7X_VMEM_LIMIT),
        name="in_proj",
    )(x_bf, x_bf, w_t, w_t, w_out)


def _t5_bucket_np(dist):
    max_exact = NUM_BUCKETS // 2
    is_small = dist < max_exact
    ratio = np.maximum(dist, max_exact).astype(np.float32) / np.float32(max_exact)
    large = max_exact + (np.log(ratio) / np.float32(math.log(MAX_DISTANCE / max_exact))
                         * np.float32(NUM_BUCKETS - max_exact)).astype(np.int32)
    large = np.minimum(large, NUM_BUCKETS - 1)
    return np.where(is_small, dist, large)


def _swa_bucket_row():
    c = np.arange(2 * BLOCK, dtype=np.int32)
    return _t5_bucket_np(np.clip(BLOCK - c, 0, MAX_DISTANCE - 1)).astype(np.int32)[None, :]


def _swa_build_bias(bucket_ref, relb_ref, tbl_ref):
    q = lax.broadcasted_iota(jnp.int32, (BLOCK, 2 * BLOCK), 0)
    s = lax.broadcasted_iota(jnp.int32, (BLOCK, 2 * BLOCK), 1)
    dist = q + BLOCK - s
    in_win = jnp.where(dist >= 0, jnp.where(dist < WINDOW, 1, 0), 0)
    bucket = bucket_ref[...]
    for h in range(SWA_HEADS):
        u = jnp.zeros((1, 2 * BLOCK), F32)
        for bkt in range(NUM_BUCKETS):
            u = jnp.where(bucket == bkt, relb_ref[bkt * SWA_HEADS + h], u)
        t = pltpu.roll(jnp.broadcast_to(u, (BLOCK, 2 * BLOCK)), 0, 1, stride=1, stride_axis=0)
        tbl_ref[h] = jnp.where(in_win == 1, t, -jnp.inf)


_SWA_YIELDS_PER_KV = 2


def _swa_block(sink_ref, q_ref, z0_ref, z1_ref, kp_ref, kc_ref, vp_ref, vc_ref, tbl_ref, first_mask, out,
               kv_heads):
    scale = SWA_HEAD_DIM ** -0.5
    half = SWA_HEADS // 2
    for kv in kv_heads:
        ksl = slice(kv * SWA_HEAD_DIM, (kv + 1) * SWA_HEAD_DIM)
        kk = jnp.concatenate([kp_ref[:, ksl], kc_ref[:, ksl]], axis=0)
        vv = jnp.concatenate([vp_ref[:, ksl], vc_ref[:, ksl]], axis=0)
        heads = [kv * SWA_GROUP + g for g in range(SWA_GROUP)]
        q4 = jnp.concatenate(
            [q_ref[:, h * SWA_HEAD_DIM:(h + 1) * SWA_HEAD_DIM] for h in heads], axis=0)
        s4 = lax.dot_general(q4, kk, _NT, preferred_element_type=F32)
        ps, denoms = [], []
        for g, h in enumerate(heads):
            s = s4[g * BLOCK:(g + 1) * BLOCK] * scale + (tbl_ref[h] + first_mask)
            sink = sink_ref[h]
            m = jnp.maximum(jnp.max(s, axis=-1, keepdims=True), sink)
            p = jnp.exp(s - m)
            denoms.append(jnp.sum(p, axis=-1, keepdims=True) + jnp.exp(sink - m))
            ps.append(p.astype(BF16))
        yield
        o4 = jnp.dot(jnp.concatenate(ps, axis=0), vv, preferred_element_type=F32)
        for g, h in enumerate(heads):
            z_ref = z0_ref if h < half else z1_ref
            zsl = slice((h % half) * SWA_HEAD_DIM, (h % half + 1) * SWA_HEAD_DIM)
            o = o4[g * BLOCK:(g + 1) * BLOCK] / denoms[g]
            out[:, h * SWA_HEAD_DIM:(h + 1) * SWA_HEAD_DIM] = (
                o * _silu(z_ref[:, zsl].astype(F32))).astype(out.dtype)
        yield


def _cumsum_lanes(x):
    n = x.shape[-1]
    lane = lax.broadcasted_iota(jnp.int32, x.shape, x.ndim - 1)
    shift = 1
    while shift < n:
        x = x + jnp.where(lane >= shift, pltpu.roll(x, shift, x.ndim - 1), 0.0)
        shift *= 2
    return x


_CONV_HALO = 8


def _mlstm_reset(xbuf, c_state, n_state, m_state):
    xbuf[0:_CONV_HALO, :] = jnp.zeros((_CONV_HALO, 2 * M_QK_W), F32)
    c_state[...] = jnp.zeros_like(c_state)
    n_state[...] = jnp.zeros_like(n_state)
    m_state[...] = jnp.zeros_like(m_state)


_MLSTM_YIELDS = 1 + 3 * M_HEADS


def _mlstm_chunk(q_ref, k_ref, v_ref, g_ref, o_ref, z_ref, cw_ref, cb_ref, gb_ref, ng_ref,
                 xbuf, c_state, n_state, m_state, out, col0):
    L = M_CHUNK
    HALO = _CONV_HALO

    xbuf[HALO:HALO + L, 0:M_QK_W] = q_ref[...].astype(F32)
    xbuf[HALO:HALO + L, M_QK_W:] = k_ref[...].astype(F32)
    acc = cb_ref[...] + cw_ref[CONV_WIDTH - 1:CONV_WIDTH, :] * xbuf[HALO:HALO + L, :]
    for j in range(CONV_WIDTH - 1):
        back = CONV_WIDTH - 1 - j
        acc = acc + cw_ref[j:j + 1, :] * xbuf[HALO - back:HALO - back + L, :]
    tail = xbuf[L:L + HALO, :]
    xbuf[0:HALO, :] = tail
    qk = _silu(acc)
    q_all = qk[:, 0:M_QK_W]
    k_all = qk[:, M_QK_W:] * (M_QK_DIM ** -0.5)

    gt = (g_ref[...] + gb_ref[...]).T[0:8, :]
    row8 = lax.broadcasted_iota(jnp.int32, (8, L), 0)
    head_rows = row8 >= M_HEADS
    log_i = jnp.where(head_rows, pltpu.roll(gt, M_HEADS, 0), 0.0)
    log_f = jnp.where(head_rows, _log_sigmoid(gt), 0.0)
    b = _cumsum_lanes(log_f)
    g_tot = b[:, L - 1:L]
    a = g_tot - b + log_i
    m_loc = jnp.max(a, axis=-1, keepdims=True)
    w = jnp.exp(a - m_loc)
    m_prev = m_state[...]
    inter = b + m_prev
    r = log_i - b

    cols = jnp.concatenate([inter, w, b, jnp.zeros((L - 24, L), F32)], axis=0).T

    g_rep = jnp.broadcast_to(g_tot, (8, L))
    m_loc_rep = jnp.broadcast_to(m_loc, (8, L))
    m_new = jnp.maximum(g_rep + m_prev, m_loc_rep)
    s_prev = jnp.exp(g_rep + m_prev - m_new)
    s_loc = jnp.exp(m_loc_rep - m_new)
    n_prev_all = n_state[...]

    ti = lax.broadcasted_iota(jnp.int32, (L, L), 0)
    si = lax.broadcasted_iota(jnp.int32, (L, L), 1)
    causal = ti >= si
    yield

    for h in range(M_HEADS):
        rr = M_HEADS + h
        qh = q_all[:, h * M_QK_DIM:(h + 1) * M_QK_DIM]
        kh = k_all[:, h * M_QK_DIM:(h + 1) * M_QK_DIM]
        vsl = slice(h * M_V_DIM, (h + 1) * M_V_DIM)
        vh = v_ref[:, vsl]
        inter_c = cols[:, rr:rr + 1]
        w_c = cols[:, 8 + rr:9 + rr]
        b_c = cols[:, 16 + rr:17 + rr]

        dmat = jnp.where(causal, b_c + r[rr:rr + 1, :], -jnp.inf)
        m_t = jnp.maximum(inter_c, jnp.max(dmat, axis=-1, keepdims=True))
        qb = qh.astype(BF16)
        kb = kh.astype(BF16)
        s = lax.dot_general(qb, kb, _NT, preferred_element_type=F32)
        sm = s * jnp.exp(dmat - m_t)
        w_inter = jnp.exp(inter_c - m_t)
        yield
        c_prev = c_state[h]
        num = (jnp.dot(sm.astype(BF16), vh, preferred_element_type=F32)
               + w_inter * jnp.dot(qb, c_prev.astype(BF16), preferred_element_type=F32))
        n_prev = n_prev_all[rr:rr + 1, :]
        den = (jnp.sum(sm, axis=-1, keepdims=True)
               + w_inter * jnp.sum(qh * n_prev, axis=-1, keepdims=True))
        hcell = num / jnp.maximum(jnp.abs(den), jnp.exp(-m_t))

        hg = _sigmoid(o_ref[:, vsl].astype(F32)) * hcell
        mu = jnp.mean(hg, axis=-1, keepdims=True)
        var = jnp.mean(jnp.square(hg - mu), axis=-1, keepdims=True)
        hn = (hg - mu) * lax.rsqrt(var + HEAD_NORM_EPS) * ng_ref[:, vsl]
        out[:, col0 + h * M_V_DIM:col0 + (h + 1) * M_V_DIM] = (
            hn * _silu(z_ref[:, vsl].astype(F32))).astype(out.dtype)
        yield

        wk = w_c * kh
        c_loc = jnp.dot(wk.T.astype(BF16), vh, preferred_element_type=F32)
        sp_row = s_prev[rr:rr + 1, :]
        sl_row = s_loc[rr:rr + 1, :]
        sp2 = jnp.concatenate([sp_row, sp_row], axis=1)
        sl2 = jnp.concatenate([sl_row, sl_row], axis=1)
        c_state[h] = sp2 * c_prev + sl2 * c_loc
        n_loc = jnp.sum(wk, axis=0, keepdims=True)
        n_state[rr:rr + 1, :] = sp_row * n_prev + sl_row * n_loc
        yield

    m_state[...] = m_new


def _mem_kv_kernel(mem_ref, w_ref, o_ref):
    o_ref[...] = jnp.dot(mem_ref[...].astype(BF16), w_ref[...].astype(BF16),
                         preferred_element_type=F32).astype(o_ref.dtype)


def _mem_kv(mem2d, w_mem_kv, *, tn=512):
    M, K = mem2d.shape
    N = w_mem_kv.shape[1]
    return pl.pallas_call(
        _mem_kv_kernel,
        out_shape=jax.ShapeDtypeStruct((M, N), BF16),
        grid=(N // tn,),
        in_specs=[pl.BlockSpec((M, K), lambda j: (0, 0)),
                  pl.BlockSpec((K, tn), lambda j: (0, j))],
        out_specs=pl.BlockSpec((M, tn), lambda j: (0, j)),
        compiler_params=pltpu.CompilerParams(
            dimension_semantics=("arbitrary",), vmem_limit_bytes=V7X_VMEM_LIMIT),
        name="mem_kv",
    )(mem2d, w_mem_kv)


_XATTN_YIELDS = 2 * X_HEADS


def _xattn_block(q_ref, z_ref, k_ref, v_ref, out, col0):
    scale = X_HEAD_DIM ** -0.5
    for h in range(X_HEADS):
        hsl = slice(h * X_HEAD_DIM, (h + 1) * X_HEAD_DIM)
        s = lax.dot_general(q_ref[:, hsl], k_ref[:, hsl], _NT, preferred_element_type=F32) * scale
        m = jnp.max(s, axis=-1, keepdims=True)
        p = jnp.exp(s - m)
        denom = jnp.sum(p, axis=-1, keepdims=True)
        yield
        o = jnp.dot(p.astype(BF16), v_ref[:, hsl], preferred_element_type=F32) / denom
        out[:, col0 + h * X_HEAD_DIM:col0 + (h + 1) * X_HEAD_DIM] = (
            o * _silu(z_ref[:, hsl].astype(F32))).astype(out.dtype)
        yield


_LANES = 128


def _fused_kernel(sink_ref, relb_ref, bucket_ref,
                  cur_ref, kvp_ref, g_ref, cw_ref, cb_ref, gb_ref, ng_ref, mem_ref,
                  w_ref, x_ref, xc_ref, lng_ref, lnb_ref,
                  o_ref,
                  mix_cur, mix_new, rbuf, s1, s2, mean_s, rstd_s, tbl, xbuf, c_state, n_state, m_state,
                  *, n_tiles, blocks_per_seq):
    i, j = pl.program_id(0), pl.program_id(1)
    nj = pl.num_programs(1)
    tm, tn = x_ref.shape
    d_model = rbuf.shape[1]
    blk = jnp.minimum(i, n_tiles - 1) * nj + j
    col = pl.multiple_of(j * tn, tn)

    seg = lambda name: cur_ref.at[:, _proj_cols(name)]
    q_ref, kc_ref, vc_ref = seg("a_q"), seg("a_k"), seg("a_v")
    az = _proj_cols("a_z")
    z0_ref = cur_ref.at[:, az.start:az.start + SWA_W // 2]
    z1_ref = cur_ref.at[:, az.start + SWA_W // 2:az.stop]
    kp_ref, vp_ref = kvp_ref.at[:, 0:SWA_KV_W], kvp_ref.at[:, SWA_KV_W:2 * SWA_KV_W]
    mq_ref, mk_ref, mv_ref, mo_ref, mz_ref = seg("m_q"), seg("m_k"), seg("m_v"), seg("m_o"), seg("m_z")
    cq_ref, cz_ref = seg("c_q"), seg("c_z")
    memk_ref, memv_ref = mem_ref.at[:, 0:X_W], mem_ref.at[:, X_W:2 * X_W]

    @pl.when((i == 0) & (j == 0))
    def _():
        _swa_build_bias(bucket_ref, relb_ref, tbl)
        mix_cur[...] = jnp.zeros_like(mix_cur)
        mix_new[...] = jnp.zeros_like(mix_new)
        rbuf[...] = jnp.zeros_like(rbuf)
        s1[...] = jnp.zeros_like(s1)
        s2[...] = jnp.zeros_like(s2)
        mean_s[...] = jnp.zeros_like(mean_s)
        rstd_s[...] = jnp.zeros_like(rstd_s)

    @pl.when((i < n_tiles) & (blk % blocks_per_seq == 0))
    def _():
        _mlstm_reset(xbuf, c_state, n_state, m_state)

    def shift():
        return jnp.broadcast_to(DEEPNORM_ALPHA * xc_ref[:, 0:1], (tm, _LANES))

    def stage_c():
        mean, rstd = mean_s[...], rstd_s[...]
        for c in range(tn // _LANES):
            sl = slice(c * _LANES, (c + 1) * _LANES)
            dsl = pl.ds(col + c * _LANES, _LANES)
            o_ref[:, sl] = (rbuf[:, dsl] - mean) * rstd * lng_ref[:, dsl] + lnb_ref[:, dsl]

    def stage_b(c, n_r, n_k):
        rc, kc = divmod(c, n_k)
        rh, kw = tm // n_r, d_model // n_k
        rsl = slice(rc * rh, (rc + 1) * rh)
        ksl = slice(kc * kw, (kc + 1) * kw)
        dsl = pl.ds(col, tn)
        part = jnp.dot(mix_cur[rsl, ksl], w_ref[ksl, :], preferred_element_type=F32)
        r = (DEEPNORM_ALPHA * x_ref[rsl, :] if kc == 0 else rbuf[rsl, dsl]) + part
        rbuf[rsl, dsl] = r
        if kc == n_k - 1:
            c0 = DEEPNORM_ALPHA * jnp.broadcast_to(xc_ref[rsl, 0:1], (rh, _LANES))
            a1, a2 = s1[rsl, :], s2[rsl, :]
            for l in range(tn // _LANES):
                d = r[:, l * _LANES:(l + 1) * _LANES] - c0
                a1 = a1 + d
                a2 = a2 + d * d
            s1[rsl, :] = a1
            s2[rsl, :] = a2

    def stage_a_pieces():
        out = mix_new.at[pl.ds(pl.multiple_of(j * BLOCK, BLOCK), BLOCK)]
        lane = lax.broadcasted_iota(jnp.int32, (1, 2 * BLOCK), 1)
        neg = jnp.where(blk % blocks_per_seq == 0, -jnp.inf, 0.0)
        first_mask = jnp.where(lane < BLOCK, neg, 0.0)
        yield from _swa_block(sink_ref, q_ref, z0_ref, z1_ref, kp_ref, kc_ref, vp_ref, vc_ref,
                              tbl, first_mask, out, range(SWA_KV_HEADS))
        yield from _mlstm_chunk(mq_ref, mk_ref, mv_ref, g_ref, mo_ref, mz_ref, cw_ref, cb_ref, gb_ref, ng_ref,
                                xbuf, c_state, n_state, m_state, out, SWA_W)
        yield from _xattn_block(cq_ref, cz_ref, memk_ref, memv_ref, out, SWA_W + M_V_W)

    n_a = _SWA_YIELDS_PER_KV * SWA_KV_HEADS + _MLSTM_YIELDS + _XATTN_YIELDS
    n_r, n_k = 4, 4
    n_b = n_r * n_k

    @pl.when(i < n_tiles)
    def _():
        stage_c()
        done_a = done_b = 0
        for _ in stage_a_pieces():
            done_a += 1
            while done_b < n_b and done_b * n_a < done_a * n_b:
                stage_b(done_b, n_r, n_k)
                done_b += 1
        assert done_a == n_a and done_b == n_b

    @pl.when(i == n_tiles)
    def _():
        stage_c()
        stage_b(0, 1, 1)

    @pl.when(i == n_tiles + 1)
    def _():
        stage_c()

    @pl.when(j == nj - 1)
    def _():
        mean_d = jnp.sum(s1[...], axis=-1, keepdims=True) * (1.0 / d_model)
        var = jnp.sum(s2[...], axis=-1, keepdims=True) * (1.0 / d_model) - mean_d * mean_d
        mean_s[...] = shift() + jnp.broadcast_to(mean_d, (tm, _LANES))
        rstd_s[...] = jnp.broadcast_to(lax.rsqrt(var + LN_EPS), (tm, _LANES))
        s1[...] = jnp.zeros_like(s1)
        s2[...] = jnp.zeros_like(s2)
        mix_cur[...] = mix_new[...]


def _fused_mix_out(proj, gates, mkv, w_out_bf, x2d, rel_bias, sinks, conv_w, conv_b, gate_bias, norm_g,
                   ln_g, ln_b, *, batch, seq, tm=512, tn=1024):
    T, D = x2d.shape
    n_tiles = T // tm
    nj = D // tn
    blocks_per_seq = seq // BLOCK
    assert tm // BLOCK == nj and T % tm == 0 and D % tn == 0 and M_CHUNK == BLOCK and MIX_W == D

    def blk(i, j):
        return jnp.minimum(i, n_tiles - 1) * nj + j

    def cur(col):
        return lambda i, j, *_: (blk(i, j), col)

    def prev(col):
        return lambda i, j, *_: (jnp.maximum(blk(i, j) - 1, 0), col)

    def mem(col):
        return lambda i, j, *_: (blk(i, j) // blocks_per_seq, col)

    def const(i, j, *_):
        return (0, 0)

    def row_b(i):
        return jnp.clip(i - 1, 0, n_tiles - 1)

    kv = _proj_cols("a_k")
    assert _proj_cols("a_v").start == kv.stop and kv.start % (2 * SWA_KV_W) == 0
    in_specs = [
        pl.BlockSpec((1, 2 * BLOCK), const),
        pl.BlockSpec((BLOCK, PROJ_W), cur(0)),
        pl.BlockSpec((BLOCK, 2 * SWA_KV_W), prev(kv.start // (2 * SWA_KV_W))),
        pl.BlockSpec((BLOCK, GATE_W), cur(0)),
        pl.BlockSpec((CONV_WIDTH, 2 * M_QK_W), const),
        pl.BlockSpec((1, 2 * M_QK_W), const),
        pl.BlockSpec((1, GATE_W), const),
        pl.BlockSpec((1, M_V_W), const),
        pl.BlockSpec((MEM_LEN, 2 * X_W), mem(0), pipeline_mode=pl.Buffered(1)),
        pl.BlockSpec((D, tn), lambda i, j, *_: (0, j)),
        pl.BlockSpec((tm, tn), lambda i, j, *_: (row_b(i), j)),
        pl.BlockSpec((tm, _LANES), lambda i, j, *_: (row_b(i), 0)),
        pl.BlockSpec((1, D), const),
        pl.BlockSpec((1, D), const),
    ]
    grid_spec = pltpu.PrefetchScalarGridSpec(
        num_scalar_prefetch=2,
        grid=(n_tiles + 2, nj),
        in_specs=in_specs,
        out_specs=pl.BlockSpec((tm, tn), lambda i, j, *_: (jnp.maximum(i - 2, 0), jnp.where(i < 2, 0, j))),
        scratch_shapes=[
            pltpu.VMEM((tm, D), BF16),
            pltpu.VMEM((tm, D), BF16),
            pltpu.VMEM((tm, D), F32),
            pltpu.VMEM((tm, _LANES), F32),
            pltpu.VMEM((tm, _LANES), F32),
            pltpu.VMEM((tm, _LANES), F32),
            pltpu.VMEM((tm, _LANES), F32),
            pltpu.VMEM((SWA_HEADS, BLOCK, 2 * BLOCK), F32),
            pltpu.VMEM((M_CHUNK + _CONV_HALO, 2 * M_QK_W), F32),
            pltpu.VMEM((M_HEADS, M_QK_DIM, M_V_DIM), F32),
            pltpu.VMEM((8, M_QK_DIM), F32),
            pltpu.VMEM((8, M_CHUNK), F32),
        ],
    )
    return pl.pallas_call(
        functools.partial(_fused_kernel, n_tiles=n_tiles, blocks_per_seq=blocks_per_seq),
        out_shape=jax.ShapeDtypeStruct((T, D), F32),
        grid_spec=grid_spec,
        compiler_params=pltpu.CompilerParams(
            dimension_semantics=("arbitrary", "arbitrary"), vmem_limit_bytes=V7X_VMEM_LIMIT),
        name="mix_out",
    )(sinks, rel_bias.reshape(-1), jnp.asarray(_swa_bucket_row()),
      proj, proj, gates, conv_w, conv_b, gate_bias, norm_g, mkv,
      w_out_bf, x2d, x2d, ln_g, ln_b)


def kernel(x, mem, w_in, conv_w, conv_b, b_i, b_f, m_norm_g, rel_bias, sinks, w_mem_kv, w_out, ln_g, ln_b):
    B, S, D = x.shape
    assert D == D_MODEL and S % BLOCK == 0 and S % M_CHUNK == 0
    T = B * S
    x2d = x.reshape(T, D)
    gate_bias = jnp.concatenate(
        [b_i.astype(F32), b_f.astype(F32), jnp.zeros((GATE_W - GATE_COLS,), F32)])[None, :]

    proj, gates, w_out_bf = _in_proj(x2d.astype(BF16), w_in.T, w_out)
    mkv = _mem_kv(mem.reshape(B * MEM_LEN, D), w_mem_kv)
    out = _fused_mix_out(proj, gates, mkv, w_out_bf, x2d, rel_bias.astype(F32), sinks.astype(F32),
                         conv_w.astype(F32), conv_b.astype(F32)[None, :], gate_bias,
                         m_norm_g.astype(F32)[None, :], ln_g.astype(F32)[None, :], ln_b.astype(F32)[None, :],
                         batch=B, seq=S)
    return out.reshape(B, S, D).astype(x.dtype)
```

```python
import functools
import itertools
import math

import jax
import jax.numpy as jnp
import numpy as np
from jax import lax
from jax.experimental import pallas as pl
from jax.experimental.pallas import tpu as pltpu

F32 = jnp.float32
BF16 = jnp.bfloat16

D_MODEL = 4096
MEM_LEN = 256
SWA_HEADS = 16
SWA_KV_HEADS = 4
SWA_GROUP = SWA_HEADS // SWA_KV_HEADS
SWA_HEAD_DIM = 128
WINDOW = 128
BLOCK = 128
M_HEADS = 4
M_QK_DIM = 128
M_V_DIM = 256
M_CHUNK = 128
CONV_WIDTH = 4
X_HEADS = 4
X_HEAD_DIM = 256
NUM_BUCKETS = 32
MAX_DISTANCE = 128

SWA_W = SWA_HEADS * SWA_HEAD_DIM
SWA_KV_W = SWA_KV_HEADS * SWA_HEAD_DIM
M_QK_W = M_HEADS * M_QK_DIM
M_V_W = M_HEADS * M_V_DIM
X_W = X_HEADS * X_HEAD_DIM
MIX_W = SWA_W + M_V_W + X_W

DEEPNORM_ALPHA = 2.0 ** 0.25
LN_EPS = 1e-5
HEAD_NORM_EPS = 1e-6

_REF_SPLITS = (
    ("a_q", SWA_W), ("a_k", SWA_KV_W), ("a_v", SWA_KV_W), ("a_z", SWA_W),
    ("m_q", M_QK_W), ("m_k", M_QK_W), ("m_v", M_V_W), ("m_i", M_HEADS), ("m_f", M_HEADS),
    ("m_o", M_V_W), ("m_z", M_V_W), ("c_q", X_W), ("c_z", X_W),
)
GATE_W = 128
V7X_VMEM_BYTES = 64 * 1024 * 1024
V7X_VMEM_LIMIT = V7X_VMEM_BYTES - 4 * 1024 * 1024


def _ref_offsets():
    offs, acc = {}, 0
    for name, width in _REF_SPLITS:
        offs[name] = acc
        acc += width
    return offs


_REF_OFFS = _ref_offsets()
HEAD_W = _REF_OFFS["m_i"]
GATE_COLS = 2 * M_HEADS
TAIL_START = HEAD_W + GATE_COLS
TAIL_W = 2 * M_V_W + 2 * X_W
PROJ_W = HEAD_W + TAIL_W
_HEAD_SEGS = ("a_q", "a_k", "a_v", "a_z", "m_q", "m_k", "m_v")


def _proj_cols(name):
    off = _REF_OFFS[name] - (0 if name in _HEAD_SEGS else GATE_COLS)
    return slice(off, off + dict(_REF_SPLITS)[name])


def _silu(z):
    return z * (1.0 / (1.0 + jnp.exp(-z)))


def _sigmoid(z):
    return 1.0 / (1.0 + jnp.exp(-z))


def _log_sigmoid(z):
    return jnp.minimum(z, 0.0) - jnp.log1p(jnp.exp(-jnp.abs(z)))


_NT = (((1,), (1,)), ((), ()))


def _in_proj_kernel(xa_ref, xb_ref, w_ref, wg_ref, wo_ref, o_ref, g_ref, wo_bf_ref, *, n_cast_steps):
    half = xa_ref.shape[0]

    @pl.when(pl.program_id(0) * pl.num_programs(1) + pl.program_id(1) < n_cast_steps)
    def _():
        wo_bf_ref[...] = wo_ref[...].astype(BF16)

    w = w_ref[...].astype(BF16)
    for r, x_ref in enumerate((xa_ref, xb_ref)):
        rows = slice(r * half, (r + 1) * half)
        o_ref[rows, :] = lax.dot_general(x_ref[...], w, _NT, preferred_element_type=F32).astype(o_ref.dtype)

    @pl.when(pl.program_id(1) == 0)
    def _():
        wg = wg_ref[...].astype(BF16)
        for r, x_ref in enumerate((xa_ref, xb_ref)):
            rows = slice(r * half, (r + 1) * half)
            g_ref[rows, :] = lax.dot_general(x_ref[...], wg, _NT, preferred_element_type=F32)


def _in_proj(x_bf, w_t, w_out, *, tm=2048, tn=512, cast_rows=64):
    T, K = x_bf.shape
    n_head, n_tail = HEAD_W // tn, TAIL_W // tn
    nj = n_head + n_tail
    assert HEAD_W % tn == 0 and TAIL_W % tn == 0 and HEAD_W % GATE_W == 0
    n_cast_steps = w_out.shape[0] // cast_rows
    assert w_out.shape[0] % cast_rows == 0 and n_cast_steps <= (T // tm) * nj

    def cast_blk(i, j):
        return jnp.minimum(i * nj + j, n_cast_steps - 1), 0

    def w_row(i, j):
        row = jnp.where(j < n_head, j * tn, TAIL_START + (j - n_head) * tn)
        return pl.multiple_of(row, math.gcd(tn, TAIL_START)), 0

    return pl.pallas_call(
        functools.partial(_in_proj_kernel, n_cast_steps=n_cast_steps),
        out_shape=(jax.ShapeDtypeStruct((T, PROJ_W), BF16), jax.ShapeDtypeStruct((T, GATE_W), F32),
                   jax.ShapeDtypeStruct(w_out.shape, BF16)),
        grid=(T // tm, nj),
        in_specs=[
            pl.BlockSpec((tm // 2, K), lambda i, j: (2 * i, 0)),
            pl.BlockSpec((tm // 2, K), lambda i, j: (2 * i + 1, 0), pipeline_mode=pl.Buffered(1)),
            pl.BlockSpec((pl.Element(tn), pl.Element(K)), w_row),
            pl.BlockSpec((GATE_W, K), lambda i, j: (HEAD_W // GATE_W, 0)),
            pl.BlockSpec((cast_rows, w_out.shape[1]), cast_blk),
        ],
        out_specs=(
            pl.BlockSpec((tm, tn), lambda i, j: (i, j)),
            pl.BlockSpec((tm, GATE_W), lambda i, j: (i, 0)),
            pl.BlockSpec((cast_rows, w_out.shape[1]), cast_blk),
        ),
        compiler_params=pltpu.CompilerParams(
            dimension_semantics=("arbitrary", "arbitrary"),
            vmem_limit_bytes=V7X_VMEM_LIMIT),
        name="in_proj",
    )(x_bf, x_bf, w_t, w_t, w_out)


def _t5_bucket_np(dist):
    max_exact = NUM_BUCKETS // 2
    is_small = dist < max_exact
    ratio = np.maximum(dist, max_exact).astype(np.float32) / np.float32(max_exact)
    large = max_exact + (np.log(ratio) / np.float32(math.log(MAX_DISTANCE / max_exact))
                         * np.float32(NUM_BUCKETS - max_exact)).astype(np.int32)
    large = np.minimum(large, NUM_BUCKETS - 1)
    return np.where(is_small, dist, large)


def _swa_bucket_row():
    c = np.arange(2 * BLOCK, dtype=np.int32)
    return _t5_bucket_np(np.clip(BLOCK - c, 0, MAX_DISTANCE - 1)).astype(np.int32)[None, :]


def _swa_build_bias(bucket_ref, relb_ref, tbl_ref):
    q = lax.broadcasted_iota(jnp.int32, (BLOCK, 2 * BLOCK), 0)
    s = lax.broadcasted_iota(jnp.int32, (BLOCK, 2 * BLOCK), 1)
    dist = q + BLOCK - s
    in_win = jnp.where(dist >= 0, jnp.where(dist < WINDOW, 1, 0), 0)
    bucket = bucket_ref[...]
    for h in range(SWA_HEADS):
        u = jnp.zeros((1, 2 * BLOCK), F32)
        for bkt in range(NUM_BUCKETS):
            u = jnp.where(bucket == bkt, relb_ref[bkt * SWA_HEADS + h], u)
        t = pltpu.roll(jnp.broadcast_to(u, (BLOCK, 2 * BLOCK)), 0, 1, stride=1, stride_axis=0)
        tbl_ref[h] = jnp.where(in_win == 1, t, -jnp.inf)


_SWA_YIELDS_PER_KV = 2


def _swa_block(sink_ref, q_ref, z0_ref, z1_ref, kp_ref, kc_ref, vp_ref, vc_ref, tbl_ref, first_mask, out,
               kv_heads):
    scale = SWA_HEAD_DIM ** -0.5
    half = SWA_HEADS // 2
    for kv in kv_heads:
        ksl = slice(kv * SWA_HEAD_DIM, (kv + 1) * SWA_HEAD_DIM)
        kk = jnp.concatenate([kp_ref[:, ksl], kc_ref[:, ksl]], axis=0)
        vv = jnp.concatenate([vp_ref[:, ksl], vc_ref[:, ksl]], axis=0)
        heads = [kv * SWA_GROUP + g for g in range(SWA_GROUP)]
        q4 = jnp.concatenate(
            [q_ref[:, h * SWA_HEAD_DIM:(h + 1) * SWA_HEAD_DIM] for h in heads], axis=0)
        s4 = lax.dot_general(q4, kk, _NT, preferred_element_type=F32)
        ps, denoms = [], []
        for g, h in enumerate(heads):
            s = s4[g * BLOCK:(g + 1) * BLOCK] * scale + (tbl_ref[h] + first_mask)
            sink = sink_ref[h]
            m = jnp.maximum(jnp.max(s, axis=-1, keepdims=True), sink)
            p = jnp.exp(s - m)
            denoms.append(jnp.sum(p, axis=-1, keepdims=True) + jnp.exp(sink - m))
            ps.append(p.astype(BF16))
        yield
        o4 = jnp.dot(jnp.concatenate(ps, axis=0), vv, preferred_element_type=F32)
        for g, h in enumerate(heads):
            z_ref = z0_ref if h < half else z1_ref
            zsl = slice((h % half) * SWA_HEAD_DIM, (h % half + 1) * SWA_HEAD_DIM)
            o = o4[g * BLOCK:(g + 1) * BLOCK] / denoms[g]
            out[:, h * SWA_HEAD_DIM:(h + 1) * SWA_HEAD_DIM] = (
                o * _silu(z_ref[:, zsl].astype(F32))).astype(out.dtype)
        yield


def _cumsum_lanes(x):
    n = x.shape[-1]
    lane = lax.broadcasted_iota(jnp.int32, x.shape, x.ndim - 1)
    shift = 1
    while shift < n:
        x = x + jnp.where(lane >= shift, pltpu.roll(x, shift, x.ndim - 1), 0.0)
        shift *= 2
    return x


_CONV_HALO = 8


def _mlstm_reset(xbuf, c_state, n_state, m_state):
    xbuf[0:_CONV_HALO, :] = jnp.zeros((_CONV_HALO, 2 * M_QK_W), F32)
    c_state[...] = jnp.zeros_like(c_state)
    n_state[...] = jnp.zeros_like(n_state)
    m_state[...] = jnp.zeros_like(m_state)


_MLSTM_YIELDS = 1 + 3 * M_HEADS


def _mlstm_chunk(q_ref, k_ref, v_ref, g_ref, o_ref, z_ref, cw_ref, cb_ref, gb_ref, ng_ref,
                 xbuf, c_state, n_state, m_state, out, col0):
    L = M_CHUNK
    HALO = _CONV_HALO

    xbuf[HALO:HALO + L, 0:M_QK_W] = q_ref[...].astype(F32)
    xbuf[HALO:HALO + L, M_QK_W:] = k_ref[...].astype(F32)
    acc = cb_ref[...] + cw_ref[CONV_WIDTH - 1:CONV_WIDTH, :] * xbuf[HALO:HALO + L, :]
    for j in range(CONV_WIDTH - 1):
        back = CONV_WIDTH - 1 - j
        acc = acc + cw_ref[j:j + 1, :] * xbuf[HALO - back:HALO - back + L, :]
    tail = xbuf[L:L + HALO, :]
    xbuf[0:HALO, :] = tail
    qk = _silu(acc)
    q_all = qk[:, 0:M_QK_W]
    k_all = qk[:, M_QK_W:] * (M_QK_DIM ** -0.5)

    gt = (g_ref[...] + gb_ref[...]).T[0:8, :]
    row8 = lax.broadcasted_iota(jnp.int32, (8, L), 0)
    head_rows = row8 >= M_HEADS
    log_i = jnp.where(head_rows, pltpu.roll(gt, M_HEADS, 0), 0.0)
    log_f = jnp.where(head_rows, _log_sigmoid(gt), 0.0)
    b = _cumsum_lanes(log_f)
    g_tot = b[:, L - 1:L]
    a = g_tot - b + log_i
    m_loc = jnp.max(a, axis=-1, keepdims=True)
    w = jnp.exp(a - m_loc)
    m_prev = m_state[...]
    inter = b + m_prev
    r = log_i - b

    cols = jnp.concatenate([inter, w, b, jnp.zeros((L - 24, L), F32)], axis=0).T

    g_rep = jnp.broadcast_to(g_tot, (8, L))
    m_loc_rep = jnp.broadcast_to(m_loc, (8, L))
    m_new = jnp.maximum(g_rep + m_prev, m_loc_rep)
    s_prev = jnp.exp(g_rep + m_prev - m_new)
    s_loc = jnp.exp(m_loc_rep - m_new)
    n_prev_all = n_state[...]

    ti = lax.broadcasted_iota(jnp.int32, (L, L), 0)
    si = lax.broadcasted_iota(jnp.int32, (L, L), 1)
    causal = ti >= si
    yield

    for h in range(M_HEADS):
        rr = M_HEADS + h
        qh = q_all[:, h * M_QK_DIM:(h + 1) * M_QK_DIM]
        kh = k_all[:, h * M_QK_DIM:(h + 1) * M_QK_DIM]
        vsl = slice(h * M_V_DIM, (h + 1) * M_V_DIM)
        vh = v_ref[:, vsl]
        inter_c = cols[:, rr:rr + 1]
        w_c = cols[:, 8 + rr:9 + rr]
        b_c = cols[:, 16 + rr:17 + rr]

        dmat = jnp.where(causal, b_c + r[rr:rr + 1, :], -jnp.inf)
        m_t = jnp.maximum(inter_c, jnp.max(dmat, axis=-1, keepdims=True))
        qb = qh.astype(BF16)
        kb = kh.astype(BF16)
        s = lax.dot_general(qb, kb, _NT, preferred_element_type=F32)
        sm = s * jnp.exp(dmat - m_t)
        w_inter = jnp.exp(inter_c - m_t)
        yield
        c_prev = c_state[h]
        num = (jnp.dot(sm.astype(BF16), vh, preferred_element_type=F32)
               + w_inter * jnp.dot(qb, c_prev.astype(BF16), preferred_element_type=F32))
        n_prev = n_prev_all[rr:rr + 1, :]
        den = (jnp.sum(sm, axis=-1, keepdims=True)
               + w_inter * jnp.sum(qh * n_prev, axis=-1, keepdims=True))
        hcell = num / jnp.maximum(jnp.abs(den), jnp.exp(-m_t))

        hg = _sigmoid(o_ref[:, vsl].astype(F32)) * hcell
        mu = jnp.mean(hg, axis=-1, keepdims=True)
        var = jnp.mean(jnp.square(hg - mu), axis=-1, keepdims=True)
        hn = (hg - mu) * lax.rsqrt(var + HEAD_NORM_EPS) * ng_ref[:, vsl]
        out[:, col0 + h * M_V_DIM:col0 + (h + 1) * M_V_DIM] = (
            hn * _silu(z_ref[:, vsl].astype(F32))).astype(out.dtype)
        yield

        wk = w_c * kh
        c_loc = jnp.dot(wk.T.astype(BF16), vh, preferred_element_type=F32)
        sp_row = s_prev[rr:rr + 1, :]
        sl_row = s_loc[rr:rr + 1, :]
        sp2 = jnp.concatenate([sp_row, sp_row], axis=1)
        sl2 = jnp.concatenate([sl_row, sl_row], axis=1)
        c_state[h] = sp2 * c_prev + sl2 * c_loc
        n_loc = jnp.sum(wk, axis=0, keepdims=True)
        n_state[rr:rr + 1, :] = sp_row * n_prev + sl_row * n_loc
        yield

    m_state[...] = m_new


def _mem_kv_kernel(mem_ref, w_ref, o_ref):
    o_ref[...] = jnp.dot(mem_ref[...].astype(BF16), w_ref[...].astype(BF16),
                         preferred_element_type=F32).astype(o_ref.dtype)


def _mem_kv(mem2d, w_mem_kv, *, tn=512):
    M, K = mem2d.shape
    N = w_mem_kv.shape[1]
    return pl.pallas_call(
        _mem_kv_kernel,
        out_shape=jax.ShapeDtypeStruct((M, N), BF16),
        grid=(N // tn,),
        in_specs=[pl.BlockSpec((M, K), lambda j: (0, 0)),
                  pl.BlockSpec((K, tn), lambda j: (0, j))],
        out_specs=pl.BlockSpec((M, tn), lambda j: (0, j)),
        compiler_params=pltpu.CompilerParams(
            dimension_semantics=("arbitrary",), vmem_limit_bytes=V7X_VMEM_LIMIT),
        name="mem_kv",
    )(mem2d, w_mem_kv)


_XATTN_YIELDS = 2 * X_HEADS


def _xattn_block(q_ref, z_ref, k_ref, v_ref, out, col0):
    scale = X_HEAD_DIM ** -0.5
    for h in range(X_HEADS):
        hsl = slice(h * X_HEAD_DIM, (h + 1) * X_HEAD_DIM)
        s = lax.dot_general(q_ref[:, hsl], k_ref[:, hsl], _NT, preferred_element_type=F32) * scale
        m = jnp.max(s, axis=-1, keepdims=True)
        p = jnp.exp(s - m)
        denom = jnp.sum(p, axis=-1, keepdims=True)
        yield
        o = jnp.dot(p.astype(BF16), v_ref[:, hsl], preferred_element_type=F32) / denom
        out[:, col0 + h * X_HEAD_DIM:col0 + (h + 1) * X_HEAD_DIM] = (
            o * _silu(z_ref[:, hsl].astype(F32))).astype(out.dtype)
        yield


def _round_robin(*gens):
    gens = list(gens)
    while gens:
        for g in list(gens):
            try:
                next(g)
            except StopIteration:
                gens.remove(g)


def _heads_kernel(sink_ref, relb_ref, bucket_ref, cur_ref, kvp_ref, g_ref, cw_ref, cb_ref, gb_ref, ng_ref, mem_ref,
                  o_ref, tbl, xbuf, c_state, n_state, m_state):
    n = pl.program_id(0)

    @pl.when(n == 0)
    def _():
        _swa_build_bias(bucket_ref, relb_ref, tbl)
        for b in range(2):
            _mlstm_reset(xbuf.at[b], c_state.at[b], n_state.at[b], m_state.at[b])

    lane = lax.broadcasted_iota(jnp.int32, (1, 2 * BLOCK), 1)
    first_mask = jnp.where(lane < BLOCK, jnp.where(n == 0, -jnp.inf, 0.0), 0.0)

    def pieces(b):
        seg = lambda name: cur_ref.at[b, :, _proj_cols(name)]
        az = _proj_cols("a_z")
        z0 = cur_ref.at[b, :, az.start:az.start + SWA_W // 2]
        z1 = cur_ref.at[b, :, az.start + SWA_W // 2:az.stop]
        rows = slice(b * MEM_LEN, (b + 1) * MEM_LEN)
        out = o_ref.at[b]
        swa = _swa_block(sink_ref, seg("a_q"), z0, z1, kvp_ref.at[b, :, 0:SWA_KV_W], seg("a_k"),
                         kvp_ref.at[b, :, SWA_KV_W:2 * SWA_KV_W], seg("a_v"), tbl, first_mask, out,
                         range(SWA_KV_HEADS))
        mlstm = _mlstm_chunk(seg("m_q"), seg("m_k"), seg("m_v"), g_ref.at[b], seg("m_o"), seg("m_z"),
                             cw_ref, cb_ref, gb_ref, ng_ref,
                             xbuf.at[b], c_state.at[b], n_state.at[b], m_state.at[b], out, SWA_W)
        xattn = _xattn_block(seg("c_q"), seg("c_z"), mem_ref.at[rows, 0:X_W], mem_ref.at[rows, X_W:2 * X_W],
                             out, SWA_W + M_V_W)
        return swa, mlstm, xattn

    swa0, mlstm0, xattn0 = pieces(0)
    swa1, mlstm1, xattn1 = pieces(1)
    _round_robin(itertools.chain(mlstm0, xattn0, swa0), itertools.chain(swa1, mlstm1, xattn1))


def _heads(proj, gates, mkv, rel_bias, sinks, conv_w, conv_b, gate_bias, norm_g, *, batch, seq):
    nb = seq // BLOCK
    assert batch == 2 and M_CHUNK == BLOCK and mkv.shape == (batch * MEM_LEN, 2 * X_W)
    kv = _proj_cols("a_k")
    assert _proj_cols("a_v").start == kv.stop and kv.start % (2 * SWA_KV_W) == 0
    kvb = kv.start // (2 * SWA_KV_W)
    cur = lambda n, *_: (0, n, 0)
    prev = lambda n, *_: (0, jnp.maximum(n - 1, 0), kvb)
    const = lambda n, *_: (0, 0)
    proj3 = proj.reshape(batch, seq, PROJ_W)
    grid_spec = pltpu.PrefetchScalarGridSpec(
        num_scalar_prefetch=2,
        grid=(nb,),
        in_specs=[
            pl.BlockSpec((1, 2 * BLOCK), const),
            pl.BlockSpec((batch, BLOCK, PROJ_W), cur),
            pl.BlockSpec((batch, BLOCK, 2 * SWA_KV_W), prev),
            pl.BlockSpec((batch, BLOCK, GATE_W), cur),
            pl.BlockSpec((CONV_WIDTH, 2 * M_QK_W), const),
            pl.BlockSpec((1, 2 * M_QK_W), const),
            pl.BlockSpec((1, GATE_W), const),
            pl.BlockSpec((1, M_V_W), const),
            pl.BlockSpec((batch * MEM_LEN, 2 * X_W), const),
        ],
        out_specs=pl.BlockSpec((batch, BLOCK, MIX_W), cur),
        scratch_shapes=[
            pltpu.VMEM((SWA_HEADS, BLOCK, 2 * BLOCK), F32),
            pltpu.VMEM((batch, M_CHUNK + _CONV_HALO, 2 * M_QK_W), F32),
            pltpu.VMEM((batch, M_HEADS, M_QK_DIM, M_V_DIM), F32),
            pltpu.VMEM((batch, 8, M_QK_DIM), F32),
            pltpu.VMEM((batch, 8, M_CHUNK), F32),
        ],
    )
    mix = pl.pallas_call(
        _heads_kernel,
        out_shape=jax.ShapeDtypeStruct((batch, seq, MIX_W), BF16),
        grid_spec=grid_spec,
        compiler_params=pltpu.CompilerParams(
            dimension_semantics=("arbitrary",), vmem_limit_bytes=V7X_VMEM_LIMIT),
        name="heads",
    )(sinks, rel_bias.reshape(-1), jnp.asarray(_swa_bucket_row()),
      proj3, proj3, gates.reshape(batch, seq, GATE_W), conv_w, conv_b, gate_bias, norm_g, mkv)
    return mix.reshape(batch * seq, MIX_W)


def _out_ln_kernel(mix_ref, w_ref, x_ref, xc_ref, lng_ref, lnb_ref, o_ref, rbuf, s1, s2, mean_s, rstd_s, *, n_tiles):
    i, j = pl.program_id(0), pl.program_id(1)
    nj = pl.num_programs(1)
    tm, tn = x_ref.shape
    d_model = rbuf.shape[1]
    col = pl.multiple_of(j * tn, tn)

    @pl.when((i == 0) & (j == 0))
    def _():
        rbuf[...] = jnp.zeros_like(rbuf)
        s1[...] = jnp.zeros_like(s1)
        s2[...] = jnp.zeros_like(s2)
        mean_s[...] = jnp.zeros_like(mean_s)
        rstd_s[...] = jnp.zeros_like(rstd_s)

    def shift():
        return jnp.broadcast_to(DEEPNORM_ALPHA * xc_ref[:, 0:1], (tm, _LANES))

    def stage_c():
        mean, rstd = mean_s[...], rstd_s[...]
        for c in range(tn // _LANES):
            sl = slice(c * _LANES, (c + 1) * _LANES)
            dsl = pl.ds(col + c * _LANES, _LANES)
            o_ref[:, sl] = (rbuf[:, dsl] - mean) * rstd * lng_ref[:, dsl] + lnb_ref[:, dsl]

    def stage_b():
        r = DEEPNORM_ALPHA * x_ref[...] + jnp.dot(mix_ref[...], w_ref[...], preferred_element_type=F32)
        rbuf[:, pl.ds(col, tn)] = r
        c0 = shift()
        a1, a2 = s1[...], s2[...]
        for l in range(tn // _LANES):
            d = r[:, l * _LANES:(l + 1) * _LANES] - c0
            a1 = a1 + d
            a2 = a2 + d * d
        s1[...] = a1
        s2[...] = a2

    @pl.when(i < n_tiles)
    def _():
        stage_c()
        stage_b()

    @pl.when(i == n_tiles)
    def _():
        stage_c()

    @pl.when(j == nj - 1)
    def _():
        mean_d = jnp.sum(s1[...], axis=-1, keepdims=True) * (1.0 / d_model)
        var = jnp.sum(s2[...], axis=-1, keepdims=True) * (1.0 / d_model) - mean_d * mean_d
        mean_s[...] = shift() + jnp.broadcast_to(mean_d, (tm, _LANES))
        rstd_s[...] = jnp.broadcast_to(lax.rsqrt(var + LN_EPS), (tm, _LANES))
        s1[...] = jnp.zeros_like(s1)
        s2[...] = jnp.zeros_like(s2)


def _out_ln(mix, w_out_bf, x2d, ln_g, ln_b, *, tm=512, tn=1024):
    T, D = x2d.shape
    n_tiles, nj = T // tm, D // tn
    assert T % tm == 0 and D % tn == 0 and mix.shape == (T, D)
    row = lambda i: jnp.minimum(i, n_tiles - 1)
    const = lambda i, j: (0, 0)
    return pl.pallas_call(
        functools.partial(_out_ln_kernel, n_tiles=n_tiles),
        out_shape=jax.ShapeDtypeStruct((T, D), F32),
        grid=(n_tiles + 1, nj),
        in_specs=[
            pl.BlockSpec((tm, D), lambda i, j: (row(i), 0)),
            pl.BlockSpec((D, tn), lambda i, j: (0, j)),
            pl.BlockSpec((tm, tn), lambda i, j: (row(i), j)),
            pl.BlockSpec((tm, _LANES), lambda i, j: (row(i), 0)),
            pl.BlockSpec((1, D), const),
            pl.BlockSpec((1, D), const),
        ],
        out_specs=pl.BlockSpec((tm, tn), lambda i, j: (jnp.maximum(i - 1, 0), jnp.where(i < 1, 0, j))),
        scratch_shapes=[
            pltpu.VMEM((tm, D), F32),
            pltpu.VMEM((tm, _LANES), F32),
            pltpu.VMEM((tm, _LANES), F32),
            pltpu.VMEM((tm, _LANES), F32),
            pltpu.VMEM((tm, _LANES), F32),
        ],
        compiler_params=pltpu.CompilerParams(
            dimension_semantics=("arbitrary", "arbitrary"), vmem_limit_bytes=V7X_VMEM_LIMIT),
        name="out_ln",
    )(mix, w_out_bf, x2d, x2d, ln_g, ln_b)


_LANES = 128


def _fused_kernel(sink_ref, relb_ref, bucket_ref,
                  cur_ref, kvp_ref, g_ref, cw_ref, cb_ref, gb_ref, ng_ref, mem_ref,
                  w_ref, x_ref, xc_ref, lng_ref, lnb_ref,
                  o_ref,
                  mix_cur, mix_new, rbuf, s1, s2, mean_s, rstd_s, tbl, xbuf, c_state, n_state, m_state,
                  *, n_tiles, blocks_per_seq):
    i, j = pl.program_id(0), pl.program_id(1)
    nj = pl.num_programs(1)
    tm, tn = x_ref.shape
    d_model = rbuf.shape[1]
    blk = jnp.minimum(i, n_tiles - 1) * nj + j
    col = pl.multiple_of(j * tn, tn)

    seg = lambda name: cur_ref.at[:, _proj_cols(name)]
    q_ref, kc_ref, vc_ref = seg("a_q"), seg("a_k"), seg("a_v")
    az = _proj_cols("a_z")
    z0_ref = cur_ref.at[:, az.start:az.start + SWA_W // 2]
    z1_ref = cur_ref.at[:, az.start + SWA_W // 2:az.stop]
    kp_ref, vp_ref = kvp_ref.at[:, 0:SWA_KV_W], kvp_ref.at[:, SWA_KV_W:2 * SWA_KV_W]
    mq_ref, mk_ref, mv_ref, mo_ref, mz_ref = seg("m_q"), seg("m_k"), seg("m_v"), seg("m_o"), seg("m_z")
    cq_ref, cz_ref = seg("c_q"), seg("c_z")
    memk_ref, memv_ref = mem_ref.at[:, 0:X_W], mem_ref.at[:, X_W:2 * X_W]

    @pl.when((i == 0) & (j == 0))
    def _():
        _swa_build_bias(bucket_ref, relb_ref, tbl)
        mix_cur[...] = jnp.zeros_like(mix_cur)
        mix_new[...] = jnp.zeros_like(mix_new)
        rbuf[...] = jnp.zeros_like(rbuf)
        s1[...] = jnp.zeros_like(s1)
        s2[...] = jnp.zeros_like(s2)
        mean_s[...] = jnp.zeros_like(mean_s)
        rstd_s[...] = jnp.zeros_like(rstd_s)

    @pl.when((i < n_tiles) & (blk % blocks_per_seq == 0))
    def _():
        _mlstm_reset(xbuf, c_state, n_state, m_state)

    def shift():
        return jnp.broadcast_to(DEEPNORM_ALPHA * xc_ref[:, 0:1], (tm, _LANES))

    def stage_c():
        mean, rstd = mean_s[...], rstd_s[...]
        for c in range(tn // _LANES):
            sl = slice(c * _LANES, (c + 1) * _LANES)
            dsl = pl.ds(col + c * _LANES, _LANES)
            o_ref[:, sl] = (rbuf[:, dsl] - mean) * rstd * lng_ref[:, dsl] + lnb_ref[:, dsl]

    def stage_b(c, n_k, n_n):
        kc, nc = divmod(c, n_n)
        kw, nw = d_model // n_k, tn // n_n
        ksl = slice(kc * kw, (kc + 1) * kw)
        nsl = slice(nc * nw, (nc + 1) * nw)
        dsl = pl.ds(col + nc * nw, nw)
        part = jnp.dot(mix_cur[:, ksl], w_ref[ksl, nsl], preferred_element_type=F32)
        r = (DEEPNORM_ALPHA * x_ref[:, nsl] if kc == 0 else rbuf[:, dsl]) + part
        rbuf[:, dsl] = r
        if kc == n_k - 1:
            c0 = shift()
            a1, a2 = s1[...], s2[...]
            for l in range(nw // _LANES):
                d = r[:, l * _LANES:(l + 1) * _LANES] - c0
                a1 = a1 + d
                a2 = a2 + d * d
            s1[...] = a1
            s2[...] = a2

    def stage_a_pieces():
        out = mix_new.at[pl.ds(pl.multiple_of(j * BLOCK, BLOCK), BLOCK)]
        lane = lax.broadcasted_iota(jnp.int32, (1, 2 * BLOCK), 1)
        neg = jnp.where(blk % blocks_per_seq == 0, -jnp.inf, 0.0)
        first_mask = jnp.where(lane < BLOCK, neg, 0.0)
        yield from _swa_block(sink_ref, q_ref, z0_ref, z1_ref, kp_ref, kc_ref, vp_ref, vc_ref,
                              tbl, first_mask, out, range(SWA_KV_HEADS))
        yield from _mlstm_chunk(mq_ref, mk_ref, mv_ref, g_ref, mo_ref, mz_ref, cw_ref, cb_ref, gb_ref, ng_ref,
                                xbuf, c_state, n_state, m_state, out, SWA_W)
        yield from _xattn_block(cq_ref, cz_ref, memk_ref, memv_ref, out, SWA_W + M_V_W)

    n_a = _SWA_YIELDS_PER_KV * SWA_KV_HEADS + _MLSTM_YIELDS + _XATTN_YIELDS
    n_n = tn // 256
    b_after = {1: (0, 1), n_a // 2: (2, 3)}

    @pl.when(i < n_tiles)
    def _():
        stage_c()
        done_a = 0
        for _ in stage_a_pieces():
            done_a += 1
            for c in b_after.get(done_a, ()):
                stage_b(c, 1, n_n)
        assert done_a == n_a

    @pl.when(i == n_tiles)
    def _():
        stage_c()
        stage_b(0, 1, 1)

    @pl.when(i == n_tiles + 1)
    def _():
        stage_c()

    @pl.when(j == nj - 1)
    def _():
        mean_d = jnp.sum(s1[...], axis=-1, keepdims=True) * (1.0 / d_model)
        var = jnp.sum(s2[...], axis=-1, keepdims=True) * (1.0 / d_model) - mean_d * mean_d
        mean_s[...] = shift() + jnp.broadcast_to(mean_d, (tm, _LANES))
        rstd_s[...] = jnp.broadcast_to(lax.rsqrt(var + LN_EPS), (tm, _LANES))
        s1[...] = jnp.zeros_like(s1)
        s2[...] = jnp.zeros_like(s2)
        mix_cur[...] = mix_new[...]


def _fused_mix_out(proj, gates, mkv, w_out_bf, x2d, rel_bias, sinks, conv_w, conv_b, gate_bias, norm_g,
                   ln_g, ln_b, *, batch, seq, tm=512, tn=1024):
    T, D = x2d.shape
    n_tiles = T // tm
    nj = D // tn
    blocks_per_seq = seq // BLOCK
    assert tm // BLOCK == nj and T % tm == 0 and D % tn == 0 and M_CHUNK == BLOCK and MIX_W == D

    def blk(i, j):
        return jnp.minimum(i, n_tiles - 1) * nj + j

    def cur(col):
        return lambda i, j, *_: (blk(i, j), col)

    def prev(col):
        return lambda i, j, *_: (jnp.maximum(blk(i, j) - 1, 0), col)

    def mem(col):
        return lambda i, j, *_: (blk(i, j) // blocks_per_seq, col)

    def const(i, j, *_):
        return (0, 0)

    def row_b(i):
        return jnp.clip(i - 1, 0, n_tiles - 1)

    kv = _proj_cols("a_k")
    assert _proj_cols("a_v").start == kv.stop and kv.start % (2 * SWA_KV_W) == 0
    in_specs = [
        pl.BlockSpec((1, 2 * BLOCK), const),
        pl.BlockSpec((BLOCK, PROJ_W), cur(0)),
        pl.BlockSpec((BLOCK, 2 * SWA_KV_W), prev(kv.start // (2 * SWA_KV_W))),
        pl.BlockSpec((BLOCK, GATE_W), cur(0)),
        pl.BlockSpec((CONV_WIDTH, 2 * M_QK_W), const),
        pl.BlockSpec((1, 2 * M_QK_W), const),
        pl.BlockSpec((1, GATE_W), const),
        pl.BlockSpec((1, M_V_W), const),
        pl.BlockSpec((MEM_LEN, 2 * X_W), mem(0), pipeline_mode=pl.Buffered(1)),
        pl.BlockSpec((D, tn), lambda i, j, *_: (0, j)),
        pl.BlockSpec((tm, tn), lambda i, j, *_: (row_b(i), j)),
        pl.BlockSpec((tm, _LANES), lambda i, j, *_: (row_b(i), 0)),
        pl.BlockSpec((1, D), const),
        pl.BlockSpec((1, D), const),
    ]
    grid_spec = pltpu.PrefetchScalarGridSpec(
        num_scalar_prefetch=2,
        grid=(n_tiles + 2, nj),
        in_specs=in_specs,
        out_specs=pl.BlockSpec((tm, tn), lambda i, j, *_: (jnp.maximum(i - 2, 0), jnp.where(i < 2, 0, j))),
        scratch_shapes=[
            pltpu.VMEM((tm, D), BF16),
            pltpu.VMEM((tm, D), BF16),
            pltpu.VMEM((tm, D), F32),
            pltpu.VMEM((tm, _LANES), F32),
            pltpu.VMEM((tm, _LANES), F32),
            pltpu.VMEM((tm, _LANES), F32),
            pltpu.VMEM((tm, _LANES), F32),
            pltpu.VMEM((SWA_HEADS, BLOCK, 2 * BLOCK), F32),
            pltpu.VMEM((M_CHUNK + _CONV_HALO, 2 * M_QK_W), F32),
            pltpu.VMEM((M_HEADS, M_QK_DIM, M_V_DIM), F32),
            pltpu.VMEM((8, M_QK_DIM), F32),
            pltpu.VMEM((8, M_CHUNK), F32),
        ],
    )
    return pl.pallas_call(
        functools.partial(_fused_kernel, n_tiles=n_tiles, blocks_per_seq=blocks_per_seq),
        out_shape=jax.ShapeDtypeStruct((T, D), F32),
        grid_spec=grid_spec,
        compiler_params=pltpu.CompilerParams(
            dimension_semantics=("arbitrary", "arbitrary"), vmem_limit_bytes=V7X_VMEM_LIMIT),
        name="mix_out",
    )(sinks, rel_bias.reshape(-1), jnp.asarray(_swa_bucket_row()),
      proj, proj, gates, conv_w, conv_b, gate_bias, norm_g, mkv,
      w_out_bf, x2d, x2d, ln_g, ln_b)


def kernel(x, mem, w_in, conv_w, conv_b, b_i, b_f, m_norm_g, rel_bias, sinks, w_mem_kv, w_out, ln_g, ln_b):
    B, S, D = x.shape
    assert D == D_MODEL and S % BLOCK == 0 and S % M_CHUNK == 0
    T = B * S
    x2d = x.reshape(T, D)
    gate_bias = jnp.concatenate(
        [b_i.astype(F32), b_f.astype(F32), jnp.zeros((GATE_W - GATE_COLS,), F32)])[None, :]

    proj, gates, w_out_bf = _in_proj(x2d.astype(BF16), w_in.T, w_out)
    mkv = _mem_kv(mem.reshape(B * MEM_LEN, D), w_mem_kv)
    mix = _heads(proj, gates, mkv, rel_bias.astype(F32), sinks.astype(F32), conv_w.astype(F32),
                 conv_b.astype(F32)[None, :], gate_bias, m_norm_g.astype(F32)[None, :], batch=B, seq=S)
    out = _out_ln(mix, w_out_bf, x2d, ln_g.astype(F32)[None, :], ln_b.astype(F32)[None, :])
    return out.reshape(B, S, D).astype(x.dtype)
```

```python
import functools
import itertools
import math

import jax
import jax.numpy as jnp
import numpy as np
from jax import lax
from jax.experimental import pallas as pl
from jax.experimental.pallas import tpu as pltpu

F32 = jnp.float32
BF16 = jnp.bfloat16

D_MODEL = 4096
MEM_LEN = 256
SWA_HEADS = 16
SWA_KV_HEADS = 4
SWA_GROUP = SWA_HEADS // SWA_KV_HEADS
SWA_HEAD_DIM = 128
WINDOW = 128
BLOCK = 128
M_HEADS = 4
M_QK_DIM = 128
M_V_DIM = 256
M_CHUNK = 128
CONV_WIDTH = 4
X_HEADS = 4
X_HEAD_DIM = 256
NUM_BUCKETS = 32
MAX_DISTANCE = 128

SWA_W = SWA_HEADS * SWA_HEAD_DIM
SWA_KV_W = SWA_KV_HEADS * SWA_HEAD_DIM
M_QK_W = M_HEADS * M_QK_DIM
M_V_W = M_HEADS * M_V_DIM
X_W = X_HEADS * X_HEAD_DIM
MIX_W = SWA_W + M_V_W + X_W

DEEPNORM_ALPHA = 2.0 ** 0.25
LN_EPS = 1e-5
HEAD_NORM_EPS = 1e-6

_REF_SPLITS = (
    ("a_q", SWA_W), ("a_k", SWA_KV_W), ("a_v", SWA_KV_W), ("a_z", SWA_W),
    ("m_q", M_QK_W), ("m_k", M_QK_W), ("m_v", M_V_W), ("m_i", M_HEADS), ("m_f", M_HEADS),
    ("m_o", M_V_W), ("m_z", M_V_W), ("c_q", X_W), ("c_z", X_W),
)
GATE_W = 128
V7X_VMEM_BYTES = 64 * 1024 * 1024
V7X_VMEM_LIMIT = V7X_VMEM_BYTES - 4 * 1024 * 1024


def _ref_offsets():
    offs, acc = {}, 0
    for name, width in _REF_SPLITS:
        offs[name] = acc
        acc += width
    return offs


_REF_OFFS = _ref_offsets()
HEAD_W = _REF_OFFS["m_i"]
GATE_COLS = 2 * M_HEADS
TAIL_START = HEAD_W + GATE_COLS
TAIL_W = 2 * M_V_W + 2 * X_W
PROJ_W = HEAD_W + TAIL_W
_HEAD_SEGS = ("a_q", "a_k", "a_v", "a_z", "m_q", "m_k", "m_v")


def _proj_cols(name):
    off = _REF_OFFS[name] - (0 if name in _HEAD_SEGS else GATE_COLS)
    return slice(off, off + dict(_REF_SPLITS)[name])


def _silu(z):
    return z * (1.0 / (1.0 + jnp.exp(-z)))


def _sigmoid(z):
    return 1.0 / (1.0 + jnp.exp(-z))


def _log_sigmoid(z):
    return jnp.minimum(z, 0.0) - jnp.log1p(jnp.exp(-jnp.abs(z)))


_NT = (((1,), (1,)), ((), ()))


def _in_proj_kernel(xa_ref, xb_ref, w_ref, wg_ref, wo_ref, o_ref, g_ref, wo_bf_ref, *, n_cast_steps):
    half = xa_ref.shape[0]

    @pl.when(pl.program_id(0) * pl.num_programs(1) + pl.program_id(1) < n_cast_steps)
    def _():
        wo_bf_ref[...] = wo_ref[...].astype(BF16)

    w = w_ref[...].astype(BF16)
    for r, x_ref in enumerate((xa_ref, xb_ref)):
        rows = slice(r * half, (r + 1) * half)
        o_ref[rows, :] = lax.dot_general(x_ref[...], w, _NT, preferred_element_type=F32).astype(o_ref.dtype)

    @pl.when(pl.program_id(1) == 0)
    def _():
        wg = wg_ref[...].astype(BF16)
        for r, x_ref in enumerate((xa_ref, xb_ref)):
            rows = slice(r * half, (r + 1) * half)
            g_ref[rows, :] = lax.dot_general(x_ref[...], wg, _NT, preferred_element_type=F32)


def _in_proj(x_bf, w_t, w_out, *, tm=2048, tn=512, cast_rows=64):
    T, K = x_bf.shape
    n_head, n_tail = HEAD_W // tn, TAIL_W // tn
    nj = n_head + n_tail
    assert HEAD_W % tn == 0 and TAIL_W % tn == 0 and HEAD_W % GATE_W == 0
    n_cast_steps = w_out.shape[0] // cast_rows
    assert w_out.shape[0] % cast_rows == 0 and n_cast_steps <= (T // tm) * nj

    def cast_blk(i, j):
        return jnp.minimum(i * nj + j, n_cast_steps - 1), 0

    def w_row(i, j):
        row = jnp.where(j < n_head, j * tn, TAIL_START + (j - n_head) * tn)
        return pl.multiple_of(row, math.gcd(tn, TAIL_START)), 0

    return pl.pallas_call(
        functools.partial(_in_proj_kernel, n_cast_steps=n_cast_steps),
        out_shape=(jax.ShapeDtypeStruct((T, PROJ_W), BF16), jax.ShapeDtypeStruct((T, GATE_W), F32),
                   jax.ShapeDtypeStruct(w_out.shape, BF16)),
        grid=(T // tm, nj),
        in_specs=[
            pl.BlockSpec((tm // 2, K), lambda i, j: (2 * i, 0)),
            pl.BlockSpec((tm // 2, K), lambda i, j: (2 * i + 1, 0), pipeline_mode=pl.Buffered(1)),
            pl.BlockSpec((pl.Element(tn), pl.Element(K)), w_row),
            pl.BlockSpec((GATE_W, K), lambda i, j: (HEAD_W // GATE_W, 0)),
            pl.BlockSpec((cast_rows, w_out.shape[1]), cast_blk),
        ],
        out_specs=(
            pl.BlockSpec((tm, tn), lambda i, j: (i, j)),
            pl.BlockSpec((tm, GATE_W), lambda i, j: (i, 0)),
            pl.BlockSpec((cast_rows, w_out.shape[1]), cast_blk),
        ),
        compiler_params=pltpu.CompilerParams(
            dimension_semantics=("arbitrary", "arbitrary"),
            vmem_limit_bytes=V7X_VMEM_LIMIT),
        name="in_proj",
    )(x_bf, x_bf, w_t, w_t, w_out)


def _t5_bucket_np(dist):
    max_exact = NUM_BUCKETS // 2
    is_small = dist < max_exact
    ratio = np.maximum(dist, max_exact).astype(np.float32) / np.float32(max_exact)
    large = max_exact + (np.log(ratio) / np.float32(math.log(MAX_DISTANCE / max_exact))
                         * np.float32(NUM_BUCKETS - max_exact)).astype(np.int32)
    large = np.minimum(large, NUM_BUCKETS - 1)
    return np.where(is_small, dist, large)


def _swa_bucket_row():
    c = np.arange(2 * BLOCK, dtype=np.int32)
    return _t5_bucket_np(np.clip(BLOCK - c, 0, MAX_DISTANCE - 1)).astype(np.int32)[None, :]


_LOG2E = math.log2(math.e)


def _swa_build_bias(bucket_ref, relb_ref, tbl_ref):
    q = lax.broadcasted_iota(jnp.int32, (BLOCK, 2 * BLOCK), 0)
    s = lax.broadcasted_iota(jnp.int32, (BLOCK, 2 * BLOCK), 1)
    dist = q + BLOCK - s
    in_win = jnp.where(dist >= 0, jnp.where(dist < WINDOW, 1, 0), 0)
    in_win_first = jnp.where(s >= BLOCK, in_win, 0)
    bucket = bucket_ref[...]
    for h in range(SWA_HEADS):
        u = jnp.zeros((1, 2 * BLOCK), F32)
        for bkt in range(NUM_BUCKETS):
            u = jnp.where(bucket == bkt, relb_ref[bkt * SWA_HEADS + h], u)
        t = pltpu.roll(jnp.broadcast_to(u * _LOG2E, (BLOCK, 2 * BLOCK)), 0, 1, stride=1, stride_axis=0)
        tbl_ref[1, h] = jnp.where(in_win == 1, t, -jnp.inf)
        tbl_ref[0, h] = jnp.where(in_win_first == 1, t, -jnp.inf)


_SWA_YIELDS_PER_KV = 2


def _swa_block(sink_ref, q_ref, z0_ref, z1_ref, kp_ref, kc_ref, vp_ref, vc_ref, tbl_ref, out, kv_heads):
    scale = SWA_HEAD_DIM ** -0.5 * _LOG2E
    half = SWA_HEADS // 2
    for kv in kv_heads:
        ksl = slice(kv * SWA_HEAD_DIM, (kv + 1) * SWA_HEAD_DIM)
        kk = jnp.concatenate([kp_ref[:, ksl], kc_ref[:, ksl]], axis=0)
        vv = jnp.concatenate([vp_ref[:, ksl], vc_ref[:, ksl]], axis=0)
        heads = [kv * SWA_GROUP + g for g in range(SWA_GROUP)]
        q4 = jnp.concatenate(
            [q_ref[:, h * SWA_HEAD_DIM:(h + 1) * SWA_HEAD_DIM] for h in heads], axis=0)
        s4 = lax.dot_general(q4, kk, _NT, preferred_element_type=F32)
        ps, denoms = [], []
        for g, h in enumerate(heads):
            s = s4[g * BLOCK:(g + 1) * BLOCK] * scale + tbl_ref[h]
            sink = sink_ref[h] * _LOG2E
            m = jnp.maximum(jnp.max(s, axis=-1, keepdims=True), sink)
            p = jnp.exp2(s - m)
            denoms.append(jnp.sum(p, axis=-1, keepdims=True) + jnp.exp2(sink - m))
            ps.append(p.astype(BF16))
        yield
        o4 = jnp.dot(jnp.concatenate(ps, axis=0), vv, preferred_element_type=F32)
        for g, h in enumerate(heads):
            z_ref = z0_ref if h < half else z1_ref
            zsl = slice((h % half) * SWA_HEAD_DIM, (h % half + 1) * SWA_HEAD_DIM)
            o = o4[g * BLOCK:(g + 1) * BLOCK] / denoms[g]
            out[:, h * SWA_HEAD_DIM:(h + 1) * SWA_HEAD_DIM] = (
                o * _silu(z_ref[:, zsl].astype(F32))).astype(out.dtype)
        yield


def _cumsum_lanes(x):
    n = x.shape[-1]
    lane = lax.broadcasted_iota(jnp.int32, x.shape, x.ndim - 1)
    shift = 1
    while shift < n:
        x = x + jnp.where(lane >= shift, pltpu.roll(x, shift, x.ndim - 1), 0.0)
        shift *= 2
    return x


_CONV_HALO = 8


def _mlstm_reset(xbuf, c_state, n_state, m_state):
    xbuf[0:_CONV_HALO, :] = jnp.zeros((_CONV_HALO, 2 * M_QK_W), F32)
    c_state[...] = jnp.zeros_like(c_state)
    n_state[...] = jnp.zeros_like(n_state)
    m_state[...] = jnp.zeros_like(m_state)


_MLSTM_YIELDS = 1 + 3 * M_HEADS


def _mlstm_chunk(q_ref, k_ref, v_ref, g_ref, o_ref, z_ref, cw_ref, cb_ref, gb_ref, ng_ref,
                 xbuf, c_state, n_state, m_state, out, col0):
    L = M_CHUNK
    HALO = _CONV_HALO

    xbuf[HALO:HALO + L, 0:M_QK_W] = q_ref[...].astype(F32)
    xbuf[HALO:HALO + L, M_QK_W:] = k_ref[...].astype(F32)
    acc = cb_ref[...] + cw_ref[CONV_WIDTH - 1:CONV_WIDTH, :] * xbuf[HALO:HALO + L, :]
    for j in range(CONV_WIDTH - 1):
        back = CONV_WIDTH - 1 - j
        acc = acc + cw_ref[j:j + 1, :] * xbuf[HALO - back:HALO - back + L, :]
    tail = xbuf[L:L + HALO, :]
    xbuf[0:HALO, :] = tail
    qk = _silu(acc)
    q_all = qk[:, 0:M_QK_W]
    k_all = qk[:, M_QK_W:] * (M_QK_DIM ** -0.5)

    gt = (g_ref[...] + gb_ref[...]).T[0:8, :]
    row8 = lax.broadcasted_iota(jnp.int32, (8, L), 0)
    head_rows = row8 >= M_HEADS
    log_i = jnp.where(head_rows, pltpu.roll(gt, M_HEADS, 0), 0.0)
    log_f = jnp.where(head_rows, _log_sigmoid(gt), 0.0)
    b = _cumsum_lanes(log_f)
    g_tot = b[:, L - 1:L]
    a = g_tot - b + log_i
    m_loc = jnp.max(a, axis=-1, keepdims=True)
    w = jnp.exp(a - m_loc)
    m_prev = m_state[...]
    inter = b + m_prev
    r = log_i - b

    cols = jnp.concatenate([inter, w, b, jnp.zeros((L - 24, L), F32)], axis=0).T

    g_rep = jnp.broadcast_to(g_tot, (8, L))
    m_loc_rep = jnp.broadcast_to(m_loc, (8, L))
    m_new = jnp.maximum(g_rep + m_prev, m_loc_rep)
    s_prev = jnp.exp(g_rep + m_prev - m_new)
    s_loc = jnp.exp(m_loc_rep - m_new)
    n_prev_all = n_state[...]

    ti = lax.broadcasted_iota(jnp.int32, (L, L), 0)
    si = lax.broadcasted_iota(jnp.int32, (L, L), 1)
    causal = ti >= si
    yield

    for h in range(M_HEADS):
        rr = M_HEADS + h
        qh = q_all[:, h * M_QK_DIM:(h + 1) * M_QK_DIM]
        kh = k_all[:, h * M_QK_DIM:(h + 1) * M_QK_DIM]
        vsl = slice(h * M_V_DIM, (h + 1) * M_V_DIM)
        vh = v_ref[:, vsl]
        inter_c = cols[:, rr:rr + 1]
        w_c = cols[:, 8 + rr:9 + rr]
        b_c = cols[:, 16 + rr:17 + rr]

        dmat = jnp.where(causal, b_c + r[rr:rr + 1, :], -jnp.inf)
        m_t = jnp.maximum(inter_c, jnp.max(dmat, axis=-1, keepdims=True))
        qb = qh.astype(BF16)
        kb = kh.astype(BF16)
        s = lax.dot_general(qb, kb, _NT, preferred_element_type=F32)
        sm = s * jnp.exp(dmat - m_t)
        w_inter = jnp.exp(inter_c - m_t)
        yield
        c_prev = c_state[h]
        num = (jnp.dot(sm.astype(BF16), vh, preferred_element_type=F32)
               + w_inter * jnp.dot(qb, c_prev.astype(BF16), preferred_element_type=F32))
        n_prev = n_prev_all[rr:rr + 1, :]
        den = (jnp.sum(sm, axis=-1, keepdims=True)
               + w_inter * jnp.sum(qh * n_prev, axis=-1, keepdims=True))
        hcell = num / jnp.maximum(jnp.abs(den), jnp.exp(-m_t))

        hg = _sigmoid(o_ref[:, vsl].astype(F32)) * hcell
        mu = jnp.mean(hg, axis=-1, keepdims=True)
        var = jnp.mean(jnp.square(hg - mu), axis=-1, keepdims=True)
        hn = (hg - mu) * lax.rsqrt(var + HEAD_NORM_EPS) * ng_ref[:, vsl]
        out[:, col0 + h * M_V_DIM:col0 + (h + 1) * M_V_DIM] = (
            hn * _silu(z_ref[:, vsl].astype(F32))).astype(out.dtype)
        yield

        wk = w_c * kh
        c_loc = jnp.dot(wk.T.astype(BF16), vh, preferred_element_type=F32)
        sp_row = s_prev[rr:rr + 1, :]
        sl_row = s_loc[rr:rr + 1, :]
        sp2 = jnp.concatenate([sp_row, sp_row], axis=1)
        sl2 = jnp.concatenate([sl_row, sl_row], axis=1)
        c_state[h] = sp2 * c_prev + sl2 * c_loc
        n_loc = jnp.sum(wk, axis=0, keepdims=True)
        n_state[rr:rr + 1, :] = sp_row * n_prev + sl_row * n_loc
        yield

    m_state[...] = m_new


def _mem_kv_kernel(mem_ref, w_ref, o_ref):
    o_ref[...] = jnp.dot(mem_ref[...].astype(BF16), w_ref[...].astype(BF16),
                         preferred_element_type=F32).astype(o_ref.dtype)


def _mem_kv(mem2d, w_mem_kv, *, tn=512):
    M, K = mem2d.shape
    N = w_mem_kv.shape[1]
    return pl.pallas_call(
        _mem_kv_kernel,
        out_shape=jax.ShapeDtypeStruct((M, N), BF16),
        grid=(N // tn,),
        in_specs=[pl.BlockSpec((M, K), lambda j: (0, 0)),
                  pl.BlockSpec((K, tn), lambda j: (0, j))],
        out_specs=pl.BlockSpec((M, tn), lambda j: (0, j)),
        compiler_params=pltpu.CompilerParams(
            dimension_semantics=("arbitrary",), vmem_limit_bytes=V7X_VMEM_LIMIT),
        name="mem_kv",
    )(mem2d, w_mem_kv)


_XATTN_YIELDS = 2 * X_HEADS


def _xattn_block(q_ref, z_ref, k_ref, v_ref, out, col0):
    scale = X_HEAD_DIM ** -0.5 * _LOG2E
    for h in range(X_HEADS):
        hsl = slice(h * X_HEAD_DIM, (h + 1) * X_HEAD_DIM)
        s = lax.dot_general(q_ref[:, hsl], k_ref[:, hsl], _NT, preferred_element_type=F32) * scale
        m = jnp.max(s, axis=-1, keepdims=True)
        p = jnp.exp2(s - m)
        denom = jnp.sum(p, axis=-1, keepdims=True)
        yield
        o = jnp.dot(p.astype(BF16), v_ref[:, hsl], preferred_element_type=F32) / denom
        out[:, col0 + h * X_HEAD_DIM:col0 + (h + 1) * X_HEAD_DIM] = (
            o * _silu(z_ref[:, hsl].astype(F32))).astype(out.dtype)
        yield


def _round_robin(*gens):
    gens = list(gens)
    while gens:
        for g in list(gens):
            try:
                next(g)
            except StopIteration:
                gens.remove(g)


def _heads_kernel(sink_ref, relb_ref, bucket_ref, cur_ref, kvp_ref, g_ref, cw_ref, cb_ref, gb_ref, ng_ref, mem_ref,
                  o_ref, tbl, xbuf, c_state, n_state, m_state):
    n = pl.program_id(0)

    @pl.when(n == 0)
    def _():
        _swa_build_bias(bucket_ref, relb_ref, tbl)
        for b in range(2):
            _mlstm_reset(xbuf.at[b], c_state.at[b], n_state.at[b], m_state.at[b])

    tbl_n = tbl.at[jnp.minimum(n, 1)]

    def pieces(b):
        seg = lambda name: cur_ref.at[b, :, _proj_cols(name)]
        az = _proj_cols("a_z")
        z0 = cur_ref.at[b, :, az.start:az.start + SWA_W // 2]
        z1 = cur_ref.at[b, :, az.start + SWA_W // 2:az.stop]
        rows = slice(b * MEM_LEN, (b + 1) * MEM_LEN)
        out = o_ref.at[b]
        swa = _swa_block(sink_ref, seg("a_q"), z0, z1, kvp_ref.at[b, :, 0:SWA_KV_W], seg("a_k"),
                         kvp_ref.at[b, :, SWA_KV_W:2 * SWA_KV_W], seg("a_v"), tbl_n, out, range(SWA_KV_HEADS))
        mlstm = _mlstm_chunk(seg("m_q"), seg("m_k"), seg("m_v"), g_ref.at[b], seg("m_o"), seg("m_z"),
                             cw_ref, cb_ref, gb_ref, ng_ref,
                             xbuf.at[b], c_state.at[b], n_state.at[b], m_state.at[b], out, SWA_W)
        xattn = _xattn_block(seg("c_q"), seg("c_z"), mem_ref.at[rows, 0:X_W], mem_ref.at[rows, X_W:2 * X_W],
                             out, SWA_W + M_V_W)
        return swa, mlstm, xattn

    swa0, mlstm0, xattn0 = pieces(0)
    swa1, mlstm1, xattn1 = pieces(1)
    _round_robin(itertools.chain(mlstm0, xattn0, swa0), itertools.chain(swa1, mlstm1, xattn1))


def _heads(proj, gates, mkv, rel_bias, sinks, conv_w, conv_b, gate_bias, norm_g, *, batch, seq):
    nb = seq // BLOCK
    assert batch == 2 and M_CHUNK == BLOCK and mkv.shape == (batch * MEM_LEN, 2 * X_W)
    kv = _proj_cols("a_k")
    assert _proj_cols("a_v").start == kv.stop and kv.start % (2 * SWA_KV_W) == 0
    kvb = kv.start // (2 * SWA_KV_W)
    cur = lambda n, *_: (0, n, 0)
    prev = lambda n, *_: (0, jnp.maximum(n - 1, 0), kvb)
    const = lambda n, *_: (0, 0)
    proj3 = proj.reshape(batch, seq, PROJ_W)
    grid_spec = pltpu.PrefetchScalarGridSpec(
        num_scalar_prefetch=2,
        grid=(nb,),
        in_specs=[
            pl.BlockSpec((1, 2 * BLOCK), const),
            pl.BlockSpec((batch, BLOCK, PROJ_W), cur),
            pl.BlockSpec((batch, BLOCK, 2 * SWA_KV_W), prev),
            pl.BlockSpec((batch, BLOCK, GATE_W), cur),
            pl.BlockSpec((CONV_WIDTH, 2 * M_QK_W), const),
            pl.BlockSpec((1, 2 * M_QK_W), const),
            pl.BlockSpec((1, GATE_W), const),
            pl.BlockSpec((1, M_V_W), const),
            pl.BlockSpec((batch * MEM_LEN, 2 * X_W), const),
        ],
        out_specs=pl.BlockSpec((batch, BLOCK, MIX_W), cur),
        scratch_shapes=[
            pltpu.VMEM((2, SWA_HEADS, BLOCK, 2 * BLOCK), F32),
            pltpu.VMEM((batch, M_CHUNK + _CONV_HALO, 2 * M_QK_W), F32),
            pltpu.VMEM((batch, M_HEADS, M_QK_DIM, M_V_DIM), F32),
            pltpu.VMEM((batch, 8, M_QK_DIM), F32),
            pltpu.VMEM((batch, 8, M_CHUNK), F32),
        ],
    )
    mix = pl.pallas_call(
        _heads_kernel,
        out_shape=jax.ShapeDtypeStruct((batch, seq, MIX_W), BF16),
        grid_spec=grid_spec,
        compiler_params=pltpu.CompilerParams(
            dimension_semantics=("arbitrary",), vmem_limit_bytes=V7X_VMEM_LIMIT),
        name="heads",
    )(sinks, rel_bias.reshape(-1), jnp.asarray(_swa_bucket_row()),
      proj3, proj3, gates.reshape(batch, seq, GATE_W), conv_w, conv_b, gate_bias, norm_g, mkv)
    return mix.reshape(batch * seq, MIX_W)


_LANES = 128


def _out_ln_kernel(mix_ref, w_ref, x_ref, xc_ref, lng_ref, lnb_ref, o_ref, rbuf, s1, s2, mean_s, rstd_s, *, n_tiles):
    i, j = pl.program_id(0), pl.program_id(1)
    nj = pl.num_programs(1)
    tm, tn = x_ref.shape
    d_model = rbuf.shape[1]
    col = pl.multiple_of(j * tn, tn)

    @pl.when((i == 0) & (j == 0))
    def _():
        rbuf[...] = jnp.zeros_like(rbuf)
        s1[...] = jnp.zeros_like(s1)
        s2[...] = jnp.zeros_like(s2)
        mean_s[...] = jnp.zeros_like(mean_s)
        rstd_s[...] = jnp.zeros_like(rstd_s)

    def shift():
        return jnp.broadcast_to(DEEPNORM_ALPHA * xc_ref[:, 0:1], (tm, _LANES))

    def stage_c():
        mean, rstd = mean_s[...], rstd_s[...]
        for c in range(tn // _LANES):
            sl = slice(c * _LANES, (c + 1) * _LANES)
            dsl = pl.ds(col + c * _LANES, _LANES)
            o_ref[:, sl] = (rbuf[:, dsl] - mean) * rstd * lng_ref[:, dsl] + lnb_ref[:, dsl]

    def stage_b():
        r = DEEPNORM_ALPHA * x_ref[...] + jnp.dot(mix_ref[...], w_ref[...], preferred_element_type=F32)
        rbuf[:, pl.ds(col, tn)] = r
        c0 = shift()
        a1, a2 = s1[...], s2[...]
        for l in range(tn // _LANES):
            d = r[:, l * _LANES:(l + 1) * _LANES] - c0
            a1 = a1 + d
            a2 = a2 + d * d
        s1[...] = a1
        s2[...] = a2

    @pl.when(i < n_tiles)
    def _():
        stage_c()
        stage_b()

    @pl.when(i == n_tiles)
    def _():
        stage_c()

    @pl.when(j == nj - 1)
    def _():
        mean_d = jnp.sum(s1[...], axis=-1, keepdims=True) * (1.0 / d_model)
        var = jnp.sum(s2[...], axis=-1, keepdims=True) * (1.0 / d_model) - mean_d * mean_d
        mean_s[...] = shift() + jnp.broadcast_to(mean_d, (tm, _LANES))
        rstd_s[...] = jnp.broadcast_to(lax.rsqrt(var + LN_EPS), (tm, _LANES))
        s1[...] = jnp.zeros_like(s1)
        s2[...] = jnp.zeros_like(s2)


def _out_ln(mix, w_out_bf, x2d, ln_g, ln_b, *, tm=1024, tn=512):
    T, D = x2d.shape
    n_tiles, nj = T // tm, D // tn
    assert T % tm == 0 and D % tn == 0 and mix.shape == (T, D)
    row = lambda i: jnp.minimum(i, n_tiles - 1)
    const = lambda i, j: (0, 0)
    return pl.pallas_call(
        functools.partial(_out_ln_kernel, n_tiles=n_tiles),
        out_shape=jax.ShapeDtypeStruct((T, D), F32),
        grid=(n_tiles + 1, nj),
        in_specs=[
            pl.BlockSpec((tm, D), lambda i, j: (row(i), 0)),
            pl.BlockSpec((D, tn), lambda i, j: (0, j)),
            pl.BlockSpec((tm, tn), lambda i, j: (row(i), j)),
            pl.BlockSpec((tm, _LANES), lambda i, j: (row(i), 0)),
            pl.BlockSpec((1, D), const),
            pl.BlockSpec((1, D), const),
        ],
        out_specs=pl.BlockSpec((tm, tn), lambda i, j: (jnp.maximum(i - 1, 0), jnp.where(i < 1, 0, j))),
        scratch_shapes=[
            pltpu.VMEM((tm, D), F32),
            pltpu.VMEM((tm, _LANES), F32),
            pltpu.VMEM((tm, _LANES), F32),
            pltpu.VMEM((tm, _LANES), F32),
            pltpu.VMEM((tm, _LANES), F32),
        ],
        compiler_params=pltpu.CompilerParams(
            dimension_semantics=("arbitrary", "arbitrary"), vmem_limit_bytes=V7X_VMEM_LIMIT),
        name="out_ln",
    )(mix, w_out_bf, x2d, x2d, ln_g, ln_b)


def kernel(x, mem, w_in, conv_w, conv_b, b_i, b_f, m_norm_g, rel_bias, sinks, w_mem_kv, w_out, ln_g, ln_b):
    B, S, D = x.shape
    assert D == D_MODEL and S % BLOCK == 0 and S % M_CHUNK == 0
    T = B * S
    x2d = x.reshape(T, D)
    gate_bias = jnp.concatenate(
        [b_i.astype(F32), b_f.astype(F32), jnp.zeros((GATE_W - GATE_COLS,), F32)])[None, :]

    proj, gates, w_out_bf = _in_proj(x2d.astype(BF16), w_in.T, w_out)
    mkv = _mem_kv(mem.reshape(B * MEM_LEN, D), w_mem_kv)
    mix = _heads(proj, gates, mkv, rel_bias.astype(F32), sinks.astype(F32), conv_w.astype(F32),
                 conv_b.astype(F32)[None, :], gate_bias, m_norm_g.astype(F32)[None, :], batch=B, seq=S)
    out = _out_ln(mix, w_out_bf, x2d, ln_g.astype(F32)[None, :], ln_b.astype(F32)[None, :])
    return out.reshape(B, S, D).astype(x.dtype)
```

```python
import functools
import itertools
import math

import jax
import jax.numpy as jnp
import numpy as np
from jax import lax
from jax.experimental import pallas as pl
from jax.experimental.pallas import tpu as pltpu

F32 = jnp.float32
BF16 = jnp.bfloat16

D_MODEL = 4096
MEM_LEN = 256
SWA_HEADS = 16
SWA_KV_HEADS = 4
SWA_GROUP = SWA_HEADS // SWA_KV_HEADS
SWA_HEAD_DIM = 128
WINDOW = 128
BLOCK = 128
M_HEADS = 4
M_QK_DIM = 128
M_V_DIM = 256
M_CHUNK = 128
CONV_WIDTH = 4
X_HEADS = 4
X_HEAD_DIM = 256
NUM_BUCKETS = 32
MAX_DISTANCE = 128

SWA_W = SWA_HEADS * SWA_HEAD_DIM
SWA_KV_W = SWA_KV_HEADS * SWA_HEAD_DIM
M_QK_W = M_HEADS * M_QK_DIM
M_V_W = M_HEADS * M_V_DIM
X_W = X_HEADS * X_HEAD_DIM
MIX_W = SWA_W + M_V_W + X_W

DEEPNORM_ALPHA = 2.0 ** 0.25
LN_EPS = 1e-5
HEAD_NORM_EPS = 1e-6

_REF_SPLITS = (
    ("a_q", SWA_W), ("a_k", SWA_KV_W), ("a_v", SWA_KV_W), ("a_z", SWA_W),
    ("m_q", M_QK_W), ("m_k", M_QK_W), ("m_v", M_V_W), ("m_i", M_HEADS), ("m_f", M_HEADS),
    ("m_o", M_V_W), ("m_z", M_V_W), ("c_q", X_W), ("c_z", X_W),
)
GATE_W = 128
V7X_VMEM_BYTES = 64 * 1024 * 1024
V7X_VMEM_LIMIT = V7X_VMEM_BYTES - 4 * 1024 * 1024


def _ref_offsets():
    offs, acc = {}, 0
    for name, width in _REF_SPLITS:
        offs[name] = acc
        acc += width
    return offs


_REF_OFFS = _ref_offsets()
HEAD_W = _REF_OFFS["m_i"]
GATE_COLS = 2 * M_HEADS
TAIL_START = HEAD_W + GATE_COLS
TAIL_W = 2 * M_V_W + 2 * X_W
PROJ_W = HEAD_W + TAIL_W
_HEAD_SEGS = ("a_q", "a_k", "a_v", "a_z", "m_q", "m_k", "m_v")


def _proj_cols(name):
    off = _REF_OFFS[name] - (0 if name in _HEAD_SEGS else GATE_COLS)
    return slice(off, off + dict(_REF_SPLITS)[name])


def _silu(z):
    return z * (1.0 / (1.0 + jnp.exp(-z)))


def _sigmoid(z):
    return 1.0 / (1.0 + jnp.exp(-z))


def _log_sigmoid(z):
    return jnp.minimum(z, 0.0) - jnp.log1p(jnp.exp(-jnp.abs(z)))


_NT = (((1,), (1,)), ((), ()))


def _in_proj_kernel(xa_ref, xb_ref, w_ref, wg_ref, wo_ref, o_ref, g_ref, wo_bf_ref, *, n_cast_steps):
    half = xa_ref.shape[0]

    @pl.when(pl.program_id(0) * pl.num_programs(1) + pl.program_id(1) < n_cast_steps)
    def _():
        wo_bf_ref[...] = wo_ref[...].astype(BF16)

    w = w_ref[...].astype(BF16)
    for r, x_ref in enumerate((xa_ref, xb_ref)):
        rows = slice(r * half, (r + 1) * half)
        o_ref[rows, :] = lax.dot_general(x_ref[...], w, _NT, preferred_element_type=F32).astype(o_ref.dtype)

    @pl.when(pl.program_id(1) == 0)
    def _():
        wg = wg_ref[...].astype(BF16)
        for r, x_ref in enumerate((xa_ref, xb_ref)):
            rows = slice(r * half, (r + 1) * half)
            g_ref[rows, :] = lax.dot_general(x_ref[...], wg, _NT, preferred_element_type=F32)


def _in_proj(x_bf, w_t, w_out, *, tm=2048, tn=512, cast_rows=64):
    T, K = x_bf.shape
    n_head, n_tail = HEAD_W // tn, TAIL_W // tn
    nj = n_head + n_tail
    assert HEAD_W % tn == 0 and TAIL_W % tn == 0 and HEAD_W % GATE_W == 0
    n_cast_steps = w_out.shape[0] // cast_rows
    assert w_out.shape[0] % cast_rows == 0 and n_cast_steps <= (T // tm) * nj

    def cast_blk(i, j):
        return jnp.minimum(i * nj + j, n_cast_steps - 1), 0

    def w_row(i, j):
        row = jnp.where(j < n_head, j * tn, TAIL_START + (j - n_head) * tn)
        return pl.multiple_of(row, math.gcd(tn, TAIL_START)), 0

    return pl.pallas_call(
        functools.partial(_in_proj_kernel, n_cast_steps=n_cast_steps),
        out_shape=(jax.ShapeDtypeStruct((T, PROJ_W), BF16), jax.ShapeDtypeStruct((T, GATE_W), F32),
                   jax.ShapeDtypeStruct(w_out.shape, BF16)),
        grid=(T // tm, nj),
        in_specs=[
            pl.BlockSpec((tm // 2, K), lambda i, j: (2 * i, 0)),
            pl.BlockSpec((tm // 2, K), lambda i, j: (2 * i + 1, 0), pipeline_mode=pl.Buffered(1)),
            pl.BlockSpec((pl.Element(tn), pl.Element(K)), w_row),
            pl.BlockSpec((GATE_W, K), lambda i, j: (HEAD_W // GATE_W, 0)),
            pl.BlockSpec((cast_rows, w_out.shape[1]), cast_blk),
        ],
        out_specs=(
            pl.BlockSpec((tm, tn), lambda i, j: (i, j)),
            pl.BlockSpec((tm, GATE_W), lambda i, j: (i, 0)),
            pl.BlockSpec((cast_rows, w_out.shape[1]), cast_blk),
        ),
        compiler_params=pltpu.CompilerParams(
            dimension_semantics=("arbitrary", "arbitrary"),
            vmem_limit_bytes=V7X_VMEM_LIMIT),
        name="in_proj",
    )(x_bf, x_bf, w_t, w_t, w_out)


def _t5_bucket_np(dist):
    max_exact = NUM_BUCKETS // 2
    is_small = dist < max_exact
    ratio = np.maximum(dist, max_exact).astype(np.float32) / np.float32(max_exact)
    large = max_exact + (np.log(ratio) / np.float32(math.log(MAX_DISTANCE / max_exact))
                         * np.float32(NUM_BUCKETS - max_exact)).astype(np.int32)
    large = np.minimum(large, NUM_BUCKETS - 1)
    return np.where(is_small, dist, large)


def _swa_bucket_row():
    c = np.arange(2 * BLOCK, dtype=np.int32)
    return _t5_bucket_np(np.clip(BLOCK - c, 0, MAX_DISTANCE - 1)).astype(np.int32)[None, :]


_LOG2E = math.log2(math.e)


def _swa_build_bias(bucket_ref, relb_ref, tbl_ref):
    q = lax.broadcasted_iota(jnp.int32, (BLOCK, 2 * BLOCK), 0)
    s = lax.broadcasted_iota(jnp.int32, (BLOCK, 2 * BLOCK), 1)
    dist = q + BLOCK - s
    in_win = jnp.where(dist >= 0, jnp.where(dist < WINDOW, 1, 0), 0)
    in_win_first = jnp.where(s >= BLOCK, in_win, 0)
    bucket = bucket_ref[...]
    for h in range(SWA_HEADS):
        u = jnp.zeros((1, 2 * BLOCK), F32)
        for bkt in range(NUM_BUCKETS):
            u = jnp.where(bucket == bkt, relb_ref[bkt * SWA_HEADS + h], u)
        t = pltpu.roll(jnp.broadcast_to(u * _LOG2E, (BLOCK, 2 * BLOCK)), 0, 1, stride=1, stride_axis=0)
        tbl_ref[1, h] = jnp.where(in_win == 1, t, -jnp.inf)
        tbl_ref[0, h] = jnp.where(in_win_first == 1, t, -jnp.inf)


_SWA_YIELDS_PER_KV = 2


def _swa_block(sink_ref, q_ref, z0_ref, z1_ref, kp_ref, kc_ref, vp_ref, vc_ref, tbl_ref, out, kv_heads):
    scale = SWA_HEAD_DIM ** -0.5 * _LOG2E
    half = SWA_HEADS // 2
    for kv in kv_heads:
        ksl = slice(kv * SWA_HEAD_DIM, (kv + 1) * SWA_HEAD_DIM)
        kk = jnp.concatenate([kp_ref[:, ksl], kc_ref[:, ksl]], axis=0)
        vv = jnp.concatenate([vp_ref[:, ksl], vc_ref[:, ksl]], axis=0)
        heads = [kv * SWA_GROUP + g for g in range(SWA_GROUP)]
        q4 = jnp.concatenate(
            [q_ref[:, h * SWA_HEAD_DIM:(h + 1) * SWA_HEAD_DIM] for h in heads], axis=0)
        s4 = lax.dot_general(q4, kk, _NT, preferred_element_type=F32)
        ps, denoms = [], []
        for g, h in enumerate(heads):
            s = s4[g * BLOCK:(g + 1) * BLOCK] * scale + tbl_ref[h]
            sink = sink_ref[h] * _LOG2E
            m = jnp.maximum(jnp.max(s, axis=-1, keepdims=True), sink)
            p = jnp.exp2(s - m)
            denoms.append(jnp.sum(p, axis=-1, keepdims=True) + jnp.exp2(sink - m))
            ps.append(p.astype(BF16))
        yield
        o4 = jnp.dot(jnp.concatenate(ps, axis=0), vv, preferred_element_type=F32)
        for g, h in enumerate(heads):
            z_ref = z0_ref if h < half else z1_ref
            zsl = slice((h % half) * SWA_HEAD_DIM, (h % half + 1) * SWA_HEAD_DIM)
            o = o4[g * BLOCK:(g + 1) * BLOCK] / denoms[g]
            out[:, h * SWA_HEAD_DIM:(h + 1) * SWA_HEAD_DIM] = (
                o * _silu(z_ref[:, zsl].astype(F32))).astype(out.dtype)
        yield


def _cumsum_lanes(x):
    n = x.shape[-1]
    lane = lax.broadcasted_iota(jnp.int32, x.shape, x.ndim - 1)
    shift = 1
    while shift < n:
        x = x + jnp.where(lane >= shift, pltpu.roll(x, shift, x.ndim - 1), 0.0)
        shift *= 2
    return x


_CONV_HALO = 8


def _mlstm_reset(xbuf, c_state, n_state, m_state):
    xbuf[0:_CONV_HALO, :] = jnp.zeros((_CONV_HALO, 2 * M_QK_W), F32)
    c_state[...] = jnp.zeros_like(c_state)
    n_state[...] = jnp.zeros_like(n_state)
    m_state[...] = jnp.zeros_like(m_state)


_MLSTM_YIELDS = 1 + 3 * M_HEADS


def _mlstm_chunk(q_ref, k_ref, v_ref, g_ref, o_ref, z_ref, cw_ref, cb_ref, gb_ref, ng_ref,
                 xbuf, c_state, n_state, m_state, out, col0):
    L = M_CHUNK
    HALO = _CONV_HALO

    xbuf[HALO:HALO + L, 0:M_QK_W] = q_ref[...].astype(F32)
    xbuf[HALO:HALO + L, M_QK_W:] = k_ref[...].astype(F32)
    acc = cb_ref[...] + cw_ref[CONV_WIDTH - 1:CONV_WIDTH, :] * xbuf[HALO:HALO + L, :]
    for j in range(CONV_WIDTH - 1):
        back = CONV_WIDTH - 1 - j
        acc = acc + cw_ref[j:j + 1, :] * xbuf[HALO - back:HALO - back + L, :]
    tail = xbuf[L:L + HALO, :]
    xbuf[0:HALO, :] = tail
    qk = _silu(acc)
    q_all = qk[:, 0:M_QK_W]
    k_all = qk[:, M_QK_W:] * (M_QK_DIM ** -0.5)

    gt = (g_ref[...] + gb_ref[...]).T[0:8, :]
    row8 = lax.broadcasted_iota(jnp.int32, (8, L), 0)
    head_rows = row8 >= M_HEADS
    log_i = jnp.where(head_rows, pltpu.roll(gt, M_HEADS, 0), 0.0)
    log_f = jnp.where(head_rows, _log_sigmoid(gt), 0.0)
    b = _cumsum_lanes(log_f)
    g_tot = b[:, L - 1:L]
    a = g_tot - b + log_i
    m_loc = jnp.max(a, axis=-1, keepdims=True)
    w = jnp.exp(a - m_loc)
    m_prev = m_state[...]
    inter = b + m_prev
    r = log_i - b

    cols = jnp.concatenate([inter, w, b, jnp.zeros((L - 24, L), F32)], axis=0).T

    g_rep = jnp.broadcast_to(g_tot, (8, L))
    m_loc_rep = jnp.broadcast_to(m_loc, (8, L))
    m_new = jnp.maximum(g_rep + m_prev, m_loc_rep)
    s_prev = jnp.exp(g_rep + m_prev - m_new)
    s_loc = jnp.exp(m_loc_rep - m_new)
    n_prev_all = n_state[...]

    ti = lax.broadcasted_iota(jnp.int32, (L, L), 0)
    si = lax.broadcasted_iota(jnp.int32, (L, L), 1)
    causal = ti >= si
    yield

    for h in range(M_HEADS):
        rr = M_HEADS + h
        qh = q_all[:, h * M_QK_DIM:(h + 1) * M_QK_DIM]
        kh = k_all[:, h * M_QK_DIM:(h + 1) * M_QK_DIM]
        vsl = slice(h * M_V_DIM, (h + 1) * M_V_DIM)
        vh = v_ref[:, vsl]
        inter_c = cols[:, rr:rr + 1]
        w_c = cols[:, 8 + rr:9 + rr]
        b_c = cols[:, 16 + rr:17 + rr]

        dmat = jnp.where(causal, b_c + r[rr:rr + 1, :], -jnp.inf)
        m_t = jnp.maximum(inter_c, jnp.max(dmat, axis=-1, keepdims=True))
        qb = qh.astype(BF16)
        kb = kh.astype(BF16)
        s = lax.dot_general(qb, kb, _NT, preferred_element_type=F32)
        sm = s * jnp.exp(dmat - m_t)
        w_inter = jnp.exp(inter_c - m_t)
        yield
        c_prev = c_state[h]
        num = (jnp.dot(sm.astype(BF16), vh, preferred_element_type=F32)
               + w_inter * jnp.dot(qb, c_prev.astype(BF16), preferred_element_type=F32))
        n_prev = n_prev_all[rr:rr + 1, :]
        den = (jnp.sum(sm, axis=-1, keepdims=True)
               + w_inter * jnp.sum(qh * n_prev, axis=-1, keepdims=True))
        hcell = num / jnp.maximum(jnp.abs(den), jnp.exp(-m_t))

        hg = _sigmoid(o_ref[:, vsl].astype(F32)) * hcell
        mu = jnp.mean(hg, axis=-1, keepdims=True)
        var = jnp.mean(jnp.square(hg - mu), axis=-1, keepdims=True)
        hn = (hg - mu) * lax.rsqrt(var + HEAD_NORM_EPS) * ng_ref[:, vsl]
        out[:, col0 + h * M_V_DIM:col0 + (h + 1) * M_V_DIM] = (
            hn * _silu(z_ref[:, vsl].astype(F32))).astype(out.dtype)
        yield

        wk = w_c * kh
        c_loc = jnp.dot(wk.T.astype(BF16), vh, preferred_element_type=F32)
        sp_row = s_prev[rr:rr + 1, :]
        sl_row = s_loc[rr:rr + 1, :]
        sp2 = jnp.concatenate([sp_row, sp_row], axis=1)
        sl2 = jnp.concatenate([sl_row, sl_row], axis=1)
        c_state[h] = sp2 * c_prev + sl2 * c_loc
        n_loc = jnp.sum(wk, axis=0, keepdims=True)
        n_state[rr:rr + 1, :] = sp_row * n_prev + sl_row * n_loc
        yield

    m_state[...] = m_new


def _mem_kv_kernel(mem_ref, w_ref, o_ref):
    o_ref[...] = jnp.dot(mem_ref[...].astype(BF16), w_ref[...].astype(BF16),
                         preferred_element_type=F32).astype(o_ref.dtype)


def _mem_kv(mem2d, w_mem_kv, *, tn=512):
    M, K = mem2d.shape
    N = w_mem_kv.shape[1]
    return pl.pallas_call(
        _mem_kv_kernel,
        out_shape=jax.ShapeDtypeStruct((M, N), BF16),
        grid=(N // tn,),
        in_specs=[pl.BlockSpec((M, K), lambda j: (0, 0)),
                  pl.BlockSpec((K, tn), lambda j: (0, j))],
        out_specs=pl.BlockSpec((M, tn), lambda j: (0, j)),
        compiler_params=pltpu.CompilerParams(
            dimension_semantics=("arbitrary",), vmem_limit_bytes=V7X_VMEM_LIMIT),
        name="mem_kv",
    )(mem2d, w_mem_kv)


_XATTN_YIELDS = 2 * X_HEADS


def _xattn_block(q_ref, z_ref, k_ref, v_ref, out, col0):
    scale = X_HEAD_DIM ** -0.5 * _LOG2E
    for h in range(X_HEADS):
        hsl = slice(h * X_HEAD_DIM, (h + 1) * X_HEAD_DIM)
        s = lax.dot_general(q_ref[:, hsl], k_ref[:, hsl], _NT, preferred_element_type=F32) * scale
        m = jnp.max(s, axis=-1, keepdims=True)
        p = jnp.exp2(s - m)
        denom = jnp.sum(p, axis=-1, keepdims=True)
        yield
        o = jnp.dot(p.astype(BF16), v_ref[:, hsl], preferred_element_type=F32) / denom
        out[:, col0 + h * X_HEAD_DIM:col0 + (h + 1) * X_HEAD_DIM] = (
            o * _silu(z_ref[:, hsl].astype(F32))).astype(out.dtype)
        yield


def _round_robin(*gens):
    gens = list(gens)
    while gens:
        for g in list(gens):
            try:
                next(g)
            except StopIteration:
                gens.remove(g)


def _heads_kernel(sink_ref, relb_ref, bucket_ref, cur_ref, kvp_ref, g_ref, cw_ref, cb_ref, gb_ref, ng_ref, mem_ref,
                  o_ref, tbl, xbuf, c_state, n_state, m_state):
    n = pl.program_id(0)

    @pl.when(n == 0)
    def _():
        _swa_build_bias(bucket_ref, relb_ref, tbl)
        for b in range(2):
            _mlstm_reset(xbuf.at[b], c_state.at[b], n_state.at[b], m_state.at[b])

    tbl_n = tbl.at[jnp.minimum(n, 1)]

    def pieces(b):
        seg = lambda name: cur_ref.at[b, :, _proj_cols(name)]
        az = _proj_cols("a_z")
        z0 = cur_ref.at[b, :, az.start:az.start + SWA_W // 2]
        z1 = cur_ref.at[b, :, az.start + SWA_W // 2:az.stop]
        rows = slice(b * MEM_LEN, (b + 1) * MEM_LEN)
        out = o_ref.at[b]
        swa = _swa_block(sink_ref, seg("a_q"), z0, z1, kvp_ref.at[b, :, 0:SWA_KV_W], seg("a_k"),
                         kvp_ref.at[b, :, SWA_KV_W:2 * SWA_KV_W], seg("a_v"), tbl_n, out, range(SWA_KV_HEADS))
        mlstm = _mlstm_chunk(seg("m_q"), seg("m_k"), seg("m_v"), g_ref.at[b], seg("m_o"), seg("m_z"),
                             cw_ref, cb_ref, gb_ref, ng_ref,
                             xbuf.at[b], c_state.at[b], n_state.at[b], m_state.at[b], out, SWA_W)
        xattn = _xattn_block(seg("c_q"), seg("c_z"), mem_ref.at[rows, 0:X_W], mem_ref.at[rows, X_W:2 * X_W],
                             out, SWA_W + M_V_W)
        return swa, mlstm, xattn

    swa0, mlstm0, xattn0 = pieces(0)
    swa1, mlstm1, xattn1 = pieces(1)
    next(mlstm0)
    next(mlstm1)
    attn = itertools.chain(swa0, xattn0, swa1, xattn1)
    _round_robin(mlstm0, attn, mlstm1, attn)


def _heads(proj, gates, mkv, rel_bias, sinks, conv_w, conv_b, gate_bias, norm_g, *, batch, seq):
    nb = seq // BLOCK
    assert batch == 2 and M_CHUNK == BLOCK and mkv.shape == (batch * MEM_LEN, 2 * X_W)
    kv = _proj_cols("a_k")
    assert _proj_cols("a_v").start == kv.stop and kv.start % (2 * SWA_KV_W) == 0
    kvb = kv.start // (2 * SWA_KV_W)
    cur = lambda n, *_: (0, n, 0)
    prev = lambda n, *_: (0, jnp.maximum(n - 1, 0), kvb)
    const = lambda n, *_: (0, 0)
    proj3 = proj.reshape(batch, seq, PROJ_W)
    grid_spec = pltpu.PrefetchScalarGridSpec(
        num_scalar_prefetch=2,
        grid=(nb,),
        in_specs=[
            pl.BlockSpec((1, 2 * BLOCK), const),
            pl.BlockSpec((batch, BLOCK, PROJ_W), cur),
            pl.BlockSpec((batch, BLOCK, 2 * SWA_KV_W), prev),
            pl.BlockSpec((batch, BLOCK, GATE_W), cur),
            pl.BlockSpec((CONV_WIDTH, 2 * M_QK_W), const),
            pl.BlockSpec((1, 2 * M_QK_W), const),
            pl.BlockSpec((1, GATE_W), const),
            pl.BlockSpec((1, M_V_W), const),
            pl.BlockSpec((batch * MEM_LEN, 2 * X_W), const),
        ],
        out_specs=pl.BlockSpec((batch, BLOCK, MIX_W), cur),
        scratch_shapes=[
            pltpu.VMEM((2, SWA_HEADS, BLOCK, 2 * BLOCK), F32),
            pltpu.VMEM((batch, M_CHUNK + _CONV_HALO, 2 * M_QK_W), F32),
            pltpu.VMEM((batch, M_HEADS, M_QK_DIM, M_V_DIM), F32),
            pltpu.VMEM((batch, 8, M_QK_DIM), F32),
            pltpu.VMEM((batch, 8, M_CHUNK), F32),
        ],
    )
    mix = pl.pallas_call(
        _heads_kernel,
        out_shape=jax.ShapeDtypeStruct((batch, seq, MIX_W), BF16),
        grid_spec=grid_spec,
        compiler_params=pltpu.CompilerParams(
            dimension_semantics=("arbitrary",), vmem_limit_bytes=V7X_VMEM_LIMIT),
        name="heads",
    )(sinks, rel_bias.reshape(-1), jnp.asarray(_swa_bucket_row()),
      proj3, proj3, gates.reshape(batch, seq, GATE_W), conv_w, conv_b, gate_bias, norm_g, mkv)
    return mix.reshape(batch * seq, MIX_W)


_LANES = 128


def _out_ln_kernel(mix_ref, w_ref, x_ref, xc_ref, lng_ref, lnb_ref, o_ref, rbuf, s1, s2, mean_s, rstd_s, *, n_tiles):
    i, j = pl.program_id(0), pl.program_id(1)
    nj = pl.num_programs(1)
    tm, tn = x_ref.shape
    d_model = rbuf.shape[1]
    col = pl.multiple_of(j * tn, tn)

    @pl.when((i == 0) & (j == 0))
    def _():
        rbuf[...] = jnp.zeros_like(rbuf)
        s1[...] = jnp.zeros_like(s1)
        s2[...] = jnp.zeros_like(s2)
        mean_s[...] = jnp.zeros_like(mean_s)
        rstd_s[...] = jnp.zeros_like(rstd_s)

    def shift():
        return jnp.broadcast_to(DEEPNORM_ALPHA * xc_ref[:, 0:1], (tm, _LANES))

    def stage_c():
        mean, rstd = mean_s[...], rstd_s[...]
        for c in range(tn // _LANES):
            sl = slice(c * _LANES, (c + 1) * _LANES)
            dsl = pl.ds(col + c * _LANES, _LANES)
            o_ref[:, sl] = (rbuf[:, dsl] - mean) * rstd * lng_ref[:, dsl] + lnb_ref[:, dsl]

    def stage_b():
        r = DEEPNORM_ALPHA * x_ref[...] + jnp.dot(mix_ref[...], w_ref[...], preferred_element_type=F32)
        rbuf[:, pl.ds(col, tn)] = r
        c0 = shift()
        a1, a2 = s1[...], s2[...]
        for l in range(tn // _LANES):
            d = r[:, l * _LANES:(l + 1) * _LANES] - c0
            a1 = a1 + d
            a2 = a2 + d * d
        s1[...] = a1
        s2[...] = a2

    @pl.when(i < n_tiles)
    def _():
        stage_c()
        stage_b()

    @pl.when(i == n_tiles)
    def _():
        stage_c()

    @pl.when(j == nj - 1)
    def _():
        mean_d = jnp.sum(s1[...], axis=-1, keepdims=True) * (1.0 / d_model)
        var = jnp.sum(s2[...], axis=-1, keepdims=True) * (1.0 / d_model) - mean_d * mean_d
        mean_s[...] = shift() + jnp.broadcast_to(mean_d, (tm, _LANES))
        rstd_s[...] = jnp.broadcast_to(lax.rsqrt(var + LN_EPS), (tm, _LANES))
        s1[...] = jnp.zeros_like(s1)
        s2[...] = jnp.zeros_like(s2)


def _out_ln(mix, w_out_bf, x2d, ln_g, ln_b, *, tm=1024, tn=512):
    T, D = x2d.shape
    n_tiles, nj = T // tm, D // tn
    assert T % tm == 0 and D % tn == 0 and mix.shape == (T, D)
    row = lambda i: jnp.minimum(i, n_tiles - 1)
    const = lambda i, j: (0, 0)
    return pl.pallas_call(
        functools.partial(_out_ln_kernel, n_tiles=n_tiles),
        out_shape=jax.ShapeDtypeStruct((T, D), F32),
        grid=(n_tiles + 1, nj),
        in_specs=[
            pl.BlockSpec((tm, D), lambda i, j: (row(i), 0)),
            pl.BlockSpec((D, tn), lambda i, j: (0, j)),
            pl.BlockSpec((tm, tn), lambda i, j: (row(i), j)),
            pl.BlockSpec((tm, _LANES), lambda i, j: (row(i), 0)),
            pl.BlockSpec((1, D), const),
            pl.BlockSpec((1, D), const),
        ],
        out_specs=pl.BlockSpec((tm, tn), lambda i, j: (jnp.maximum(i - 1, 0), jnp.where(i < 1, 0, j))),
        scratch_shapes=[
            pltpu.VMEM((tm, D), F32),
            pltpu.VMEM((tm, _LANES), F32),
            pltpu.VMEM((tm, _LANES), F32),
            pltpu.VMEM((tm, _LANES), F32),
            pltpu.VMEM((tm, _LANES), F32),
        ],
        compiler_params=pltpu.CompilerParams(
            dimension_semantics=("arbitrary", "arbitrary"), vmem_limit_bytes=V7X_VMEM_LIMIT),
        name="out_ln",
    )(mix, w_out_bf, x2d, x2d, ln_g, ln_b)


def kernel(x, mem, w_in, conv_w, conv_b, b_i, b_f, m_norm_g, rel_bias, sinks, w_mem_kv, w_out, ln_g, ln_b):
    B, S, D = x.shape
    assert D == D_MODEL and S % BLOCK == 0 and S % M_CHUNK == 0
    T = B * S
    x2d = x.reshape(T, D)
    gate_bias = jnp.concatenate(
        [b_i.astype(F32), b_f.astype(F32), jnp.zeros((GATE_W - GATE_COLS,), F32)])[None, :]

    proj, gates, w_out_bf = _in_proj(x2d.astype(BF16), w_in.T, w_out)
    mkv = _mem_kv(mem.reshape(B * MEM_LEN, D), w_mem_kv)
    mix = _heads(proj, gates, mkv, rel_bias.astype(F32), sinks.astype(F32), conv_w.astype(F32),
                 conv_b.astype(F32)[None, :], gate_bias, m_norm_g.astype(F32)[None, :], batch=B, seq=S)
    out = _out_ln(mix, w_out_bf, x2d, ln_g.astype(F32)[None, :], ln_b.astype(F32)[None, :])
    return out.reshape(B, S, D).astype(x.dtype)
```

```python
import functools
import itertools
import math

import jax
import jax.numpy as jnp
import numpy as np
from jax import lax
from jax.experimental import pallas as pl
from jax.experimental.pallas import tpu as pltpu

F32 = jnp.float32
BF16 = jnp.bfloat16

D_MODEL = 4096
MEM_LEN = 256
SWA_HEADS = 16
SWA_KV_HEADS = 4
SWA_GROUP = SWA_HEADS // SWA_KV_HEADS
SWA_HEAD_DIM = 128
WINDOW = 128
BLOCK = 128
M_HEADS = 4
M_QK_DIM = 128
M_V_DIM = 256
M_CHUNK = 128
CONV_WIDTH = 4
X_HEADS = 4
X_HEAD_DIM = 256
NUM_BUCKETS = 32
MAX_DISTANCE = 128

SWA_W = SWA_HEADS * SWA_HEAD_DIM
SWA_KV_W = SWA_KV_HEADS * SWA_HEAD_DIM
M_QK_W = M_HEADS * M_QK_DIM
M_V_W = M_HEADS * M_V_DIM
X_W = X_HEADS * X_HEAD_DIM
MIX_W = SWA_W + M_V_W + X_W

DEEPNORM_ALPHA = 2.0 ** 0.25
LN_EPS = 1e-5
HEAD_NORM_EPS = 1e-6

_REF_SPLITS = (
    ("a_q", SWA_W), ("a_k", SWA_KV_W), ("a_v", SWA_KV_W), ("a_z", SWA_W),
    ("m_q", M_QK_W), ("m_k", M_QK_W), ("m_v", M_V_W), ("m_i", M_HEADS), ("m_f", M_HEADS),
    ("m_o", M_V_W), ("m_z", M_V_W), ("c_q", X_W), ("c_z", X_W),
)
GATE_W = 128
V7X_VMEM_BYTES = 64 * 1024 * 1024
V7X_VMEM_LIMIT = V7X_VMEM_BYTES - 4 * 1024 * 1024


def _ref_offsets():
    offs, acc = {}, 0
    for name, width in _REF_SPLITS:
        offs[name] = acc
        acc += width
    return offs


_REF_OFFS = _ref_offsets()
HEAD_W = _REF_OFFS["m_i"]
GATE_COLS = 2 * M_HEADS
TAIL_START = HEAD_W + GATE_COLS
TAIL_W = 2 * M_V_W + 2 * X_W
PROJ_W = HEAD_W + TAIL_W
_HEAD_SEGS = ("a_q", "a_k", "a_v", "a_z", "m_q", "m_k", "m_v")


def _proj_cols(name):
    off = _REF_OFFS[name] - (0 if name in _HEAD_SEGS else GATE_COLS)
    return slice(off, off + dict(_REF_SPLITS)[name])


def _silu(z):
    return z * (1.0 / (1.0 + jnp.exp(-z)))


def _sigmoid(z):
    return 1.0 / (1.0 + jnp.exp(-z))


def _log_sigmoid(z):
    return jnp.minimum(z, 0.0) - jnp.log1p(jnp.exp(-jnp.abs(z)))


_NT = (((1,), (1,)), ((), ()))


def _in_proj_kernel(xa_ref, xb_ref, w_ref, wg_ref, wo_ref, o_ref, g_ref, wo_bf_ref, *, n_cast_steps):
    half = xa_ref.shape[0]

    @pl.when(pl.program_id(0) * pl.num_programs(1) + pl.program_id(1) < n_cast_steps)
    def _():
        wo_bf_ref[...] = wo_ref[...].astype(BF16)

    w = w_ref[...].astype(BF16)
    for r, x_ref in enumerate((xa_ref, xb_ref)):
        rows = slice(r * half, (r + 1) * half)
        o_ref[rows, :] = lax.dot_general(x_ref[...], w, _NT, preferred_element_type=F32).astype(o_ref.dtype)

    @pl.when(pl.program_id(1) == 0)
    def _():
        wg = wg_ref[...].astype(BF16)
        for r, x_ref in enumerate((xa_ref, xb_ref)):
            rows = slice(r * half, (r + 1) * half)
            g_ref[rows, :] = lax.dot_general(x_ref[...], wg, _NT, preferred_element_type=F32)


def _in_proj(x_bf, w_t, w_out, *, tm=2048, tn=512, cast_rows=64):
    T, K = x_bf.shape
    n_head, n_tail = HEAD_W // tn, TAIL_W // tn
    nj = n_head + n_tail
    assert HEAD_W % tn == 0 and TAIL_W % tn == 0 and HEAD_W % GATE_W == 0
    n_cast_steps = w_out.shape[0] // cast_rows
    assert w_out.shape[0] % cast_rows == 0 and n_cast_steps <= (T // tm) * nj

    def cast_blk(i, j):
        return jnp.minimum(i * nj + j, n_cast_steps - 1), 0

    def w_row(i, j):
        row = jnp.where(j < n_head, j * tn, TAIL_START + (j - n_head) * tn)
        return pl.multiple_of(row, math.gcd(tn, TAIL_START)), 0

    return pl.pallas_call(
        functools.partial(_in_proj_kernel, n_cast_steps=n_cast_steps),
        out_shape=(jax.ShapeDtypeStruct((T, PROJ_W), BF16), jax.ShapeDtypeStruct((T, GATE_W), F32),
                   jax.ShapeDtypeStruct(w_out.shape, BF16)),
        grid=(T // tm, nj),
        in_specs=[
            pl.BlockSpec((tm // 2, K), lambda i, j: (2 * i, 0)),
            pl.BlockSpec((tm // 2, K), lambda i, j: (2 * i + 1, 0), pipeline_mode=pl.Buffered(1)),
            pl.BlockSpec((pl.Element(tn), pl.Element(K)), w_row),
            pl.BlockSpec((GATE_W, K), lambda i, j: (HEAD_W // GATE_W, 0)),
            pl.BlockSpec((cast_rows, w_out.shape[1]), cast_blk),
        ],
        out_specs=(
            pl.BlockSpec((tm, tn), lambda i, j: (i, j)),
            pl.BlockSpec((tm, GATE_W), lambda i, j: (i, 0)),
            pl.BlockSpec((cast_rows, w_out.shape[1]), cast_blk),
        ),
        compiler_params=pltpu.CompilerParams(
            dimension_semantics=("arbitrary", "arbitrary"),
            vmem_limit_bytes=V7X_VMEM_LIMIT),
        name="in_proj",
    )(x_bf, x_bf, w_t, w_t, w_out)


def _t5_bucket_np(dist):
    max_exact = NUM_BUCKETS // 2
    is_small = dist < max_exact
    ratio = np.maximum(dist, max_exact).astype(np.float32) / np.float32(max_exact)
    large = max_exact + (np.log(ratio) / np.float32(math.log(MAX_DISTANCE / max_exact))
                         * np.float32(NUM_BUCKETS - max_exact)).astype(np.int32)
    large = np.minimum(large, NUM_BUCKETS - 1)
    return np.where(is_small, dist, large)


def _swa_bucket_row():
    c = np.arange(2 * BLOCK, dtype=np.int32)
    return _t5_bucket_np(np.clip(BLOCK - c, 0, MAX_DISTANCE - 1)).astype(np.int32)[None, :]


_LOG2E = math.log2(math.e)


def _swa_build_bias(bucket_ref, relb_ref, tbl_ref):
    q = lax.broadcasted_iota(jnp.int32, (BLOCK, 2 * BLOCK), 0)
    s = lax.broadcasted_iota(jnp.int32, (BLOCK, 2 * BLOCK), 1)
    dist = q + BLOCK - s
    in_win = jnp.where(dist >= 0, jnp.where(dist < WINDOW, 1, 0), 0)
    in_win_first = jnp.where(s >= BLOCK, in_win, 0)
    bucket = bucket_ref[...]
    for h in range(SWA_HEADS):
        u = jnp.zeros((1, 2 * BLOCK), F32)
        for bkt in range(NUM_BUCKETS):
            u = jnp.where(bucket == bkt, relb_ref[bkt * SWA_HEADS + h], u)
        t = pltpu.roll(jnp.broadcast_to(u * _LOG2E, (BLOCK, 2 * BLOCK)), 0, 1, stride=1, stride_axis=0)
        tbl_ref[1, h] = jnp.where(in_win == 1, t, -jnp.inf)
        tbl_ref[0, h] = jnp.where(in_win_first == 1, t, -jnp.inf)


_SWA_YIELDS_PER_KV = 2


def _swa_block(sink_ref, q_ref, z0_ref, z1_ref, kp_ref, kc_ref, vp_ref, vc_ref, tbl_ref, out, kv_heads):
    scale = SWA_HEAD_DIM ** -0.5 * _LOG2E
    half = SWA_HEADS // 2
    for kv in kv_heads:
        ksl = slice(kv * SWA_HEAD_DIM, (kv + 1) * SWA_HEAD_DIM)
        kk = jnp.concatenate([kp_ref[:, ksl], kc_ref[:, ksl]], axis=0)
        vv = jnp.concatenate([vp_ref[:, ksl], vc_ref[:, ksl]], axis=0)
        vv1 = jnp.concatenate([vv, jnp.ones((2 * BLOCK, SWA_HEAD_DIM), BF16)], axis=1)
        heads = [kv * SWA_GROUP + g for g in range(SWA_GROUP)]
        q4 = jnp.concatenate(
            [q_ref[:, h * SWA_HEAD_DIM:(h + 1) * SWA_HEAD_DIM] for h in heads], axis=0)
        s4 = lax.dot_general(q4, kk, _NT, preferred_element_type=F32)
        ps, sink_terms = [], []
        for g, h in enumerate(heads):
            s = s4[g * BLOCK:(g + 1) * BLOCK] * scale + tbl_ref[h]
            sink = sink_ref[h] * _LOG2E
            m = jnp.maximum(jnp.max(s, axis=-1, keepdims=True), sink)
            ps.append(jnp.exp2(s - m).astype(BF16))
            sink_terms.append(jnp.exp2(sink - m))
        yield
        o4 = jnp.dot(jnp.concatenate(ps, axis=0), vv1, preferred_element_type=F32)
        for g, h in enumerate(heads):
            z_ref = z0_ref if h < half else z1_ref
            zsl = slice((h % half) * SWA_HEAD_DIM, (h % half + 1) * SWA_HEAD_DIM)
            og = o4[g * BLOCK:(g + 1) * BLOCK]
            o = og[:, 0:SWA_HEAD_DIM] / (og[:, SWA_HEAD_DIM:] + sink_terms[g])
            out[:, h * SWA_HEAD_DIM:(h + 1) * SWA_HEAD_DIM] = (
                o * _silu(z_ref[:, zsl].astype(F32))).astype(out.dtype)
        yield


def _cumsum_lanes(x):
    n = x.shape[-1]
    lane = lax.broadcasted_iota(jnp.int32, x.shape, x.ndim - 1)
    shift = 1
    while shift < n:
        x = x + jnp.where(lane >= shift, pltpu.roll(x, shift, x.ndim - 1), 0.0)
        shift *= 2
    return x


def _conv_shift_matrix():
    L = M_CHUNK
    s = np.zeros(((CONV_WIDTH - 1) * L, 2 * L), np.float32)
    for d in range(1, CONV_WIDTH):
        t = np.arange(L)
        s[(d - 1) * L + t, L + t - d] = 1.0
    return s


def _mlstm_reset(xprev, c_state, n_state, m_state):
    xprev[...] = jnp.zeros_like(xprev)
    c_state[...] = jnp.zeros_like(c_state)
    n_state[...] = jnp.zeros_like(n_state)
    m_state[...] = jnp.zeros_like(m_state)


_MLSTM_YIELDS = 1 + 3 * M_HEADS


def _mlstm_chunk(q_ref, k_ref, v_ref, g_ref, o_ref, z_ref, cw_ref, cb_ref, gb_ref, ng_ref, shift_ref,
                 xprev, c_state, n_state, m_state, out, col0):
    L = M_CHUNK

    cur = jnp.concatenate([q_ref[...], k_ref[...]], axis=1)
    shifted = jnp.dot(shift_ref[...], jnp.concatenate([xprev[...], cur], axis=0),
                      preferred_element_type=F32)
    xprev[...] = cur
    acc = cb_ref[...] + cw_ref[CONV_WIDTH - 1:CONV_WIDTH, :] * cur.astype(F32)
    for j in range(CONV_WIDTH - 1):
        back = CONV_WIDTH - 1 - j
        acc = acc + cw_ref[j:j + 1, :] * shifted[(back - 1) * L:back * L, :]
    qk = _silu(acc)
    q_all = qk[:, 0:M_QK_W]
    k_all = qk[:, M_QK_W:] * (M_QK_DIM ** -0.5)

    gt = (g_ref[...] + gb_ref[...]).T[0:8, :]
    row8 = lax.broadcasted_iota(jnp.int32, (8, L), 0)
    head_rows = row8 >= M_HEADS
    log_i = jnp.where(head_rows, pltpu.roll(gt, M_HEADS, 0), 0.0)
    log_f = jnp.where(head_rows, _log_sigmoid(gt), 0.0)
    b = _cumsum_lanes(log_f)
    g_tot = b[:, L - 1:L]
    a = g_tot - b + log_i
    m_loc = jnp.max(a, axis=-1, keepdims=True)
    w = jnp.exp(a - m_loc)
    m_prev = m_state[...]
    inter = b + m_prev
    r = log_i - b

    cols = jnp.concatenate([inter, w, b, jnp.zeros((L - 24, L), F32)], axis=0).T

    g_rep = jnp.broadcast_to(g_tot, (8, L))
    m_loc_rep = jnp.broadcast_to(m_loc, (8, L))
    m_new = jnp.maximum(g_rep + m_prev, m_loc_rep)
    s_prev = jnp.exp(g_rep + m_prev - m_new)
    s_loc = jnp.exp(m_loc_rep - m_new)
    n_prev_all = n_state[...]

    ti = lax.broadcasted_iota(jnp.int32, (L, L), 0)
    si = lax.broadcasted_iota(jnp.int32, (L, L), 1)
    causal = ti >= si
    yield

    for h in range(M_HEADS):
        rr = M_HEADS + h
        qh = q_all[:, h * M_QK_DIM:(h + 1) * M_QK_DIM]
        kh = k_all[:, h * M_QK_DIM:(h + 1) * M_QK_DIM]
        vsl = slice(h * M_V_DIM, (h + 1) * M_V_DIM)
        vh = v_ref[:, vsl]
        inter_c = cols[:, rr:rr + 1]
        w_c = cols[:, 8 + rr:9 + rr]
        b_c = cols[:, 16 + rr:17 + rr]

        dmat = jnp.where(causal, b_c + r[rr:rr + 1, :], -jnp.inf)
        m_t = jnp.maximum(inter_c, jnp.max(dmat, axis=-1, keepdims=True))
        qb = qh.astype(BF16)
        kb = kh.astype(BF16)
        s = lax.dot_general(qb, kb, _NT, preferred_element_type=F32)
        sm = s * jnp.exp(dmat - m_t)
        w_inter = jnp.exp(inter_c - m_t)
        yield
        c_prev = c_state[h]
        num = (jnp.dot(sm.astype(BF16), vh, preferred_element_type=F32)
               + w_inter * jnp.dot(qb, c_prev.astype(BF16), preferred_element_type=F32))
        n_prev = n_prev_all[rr:rr + 1, :]
        den = (jnp.sum(sm, axis=-1, keepdims=True)
               + w_inter * jnp.sum(qh * n_prev, axis=-1, keepdims=True))
        hcell = num / jnp.maximum(jnp.abs(den), jnp.exp(-m_t))

        hg = _sigmoid(o_ref[:, vsl].astype(F32)) * hcell
        mu = jnp.mean(hg, axis=-1, keepdims=True)
        var = jnp.mean(jnp.square(hg - mu), axis=-1, keepdims=True)
        hn = (hg - mu) * lax.rsqrt(var + HEAD_NORM_EPS) * ng_ref[:, vsl]
        out[:, col0 + h * M_V_DIM:col0 + (h + 1) * M_V_DIM] = (
            hn * _silu(z_ref[:, vsl].astype(F32))).astype(out.dtype)
        yield

        wk = w_c * kh
        c_loc = jnp.dot(wk.T.astype(BF16), vh, preferred_element_type=F32)
        sp_row = s_prev[rr:rr + 1, :]
        sl_row = s_loc[rr:rr + 1, :]
        sp2 = jnp.concatenate([sp_row, sp_row], axis=1)
        sl2 = jnp.concatenate([sl_row, sl_row], axis=1)
        c_state[h] = sp2 * c_prev + sl2 * c_loc
        n_loc = jnp.sum(wk, axis=0, keepdims=True)
        n_state[rr:rr + 1, :] = sp_row * n_prev + sl_row * n_loc
        yield

    m_state[...] = m_new


def _mem_kv_kernel(mem_ref, w_ref, o_ref):
    o_ref[...] = jnp.dot(mem_ref[...].astype(BF16), w_ref[...].astype(BF16),
                         preferred_element_type=F32).astype(o_ref.dtype)


def _mem_kv(mem2d, w_mem_kv, *, tn=512):
    M, K = mem2d.shape
    N = w_mem_kv.shape[1]
    return pl.pallas_call(
        _mem_kv_kernel,
        out_shape=jax.ShapeDtypeStruct((M, N), BF16),
        grid=(N // tn,),
        in_specs=[pl.BlockSpec((M, K), lambda j: (0, 0)),
                  pl.BlockSpec((K, tn), lambda j: (0, j))],
        out_specs=pl.BlockSpec((M, tn), lambda j: (0, j)),
        compiler_params=pltpu.CompilerParams(
            dimension_semantics=("arbitrary",), vmem_limit_bytes=V7X_VMEM_LIMIT),
        name="mem_kv",
    )(mem2d, w_mem_kv)


_XATTN_YIELDS = 2 * X_HEADS


def _xattn_block(q_ref, z_ref, k_ref, v_ref, out, col0):
    scale = X_HEAD_DIM ** -0.5 * _LOG2E
    for h in range(X_HEADS):
        hsl = slice(h * X_HEAD_DIM, (h + 1) * X_HEAD_DIM)
        s = lax.dot_general(q_ref[:, hsl], k_ref[:, hsl], _NT, preferred_element_type=F32) * scale
        m = jnp.max(s, axis=-1, keepdims=True)
        p = jnp.exp2(s - m)
        denom = jnp.sum(p, axis=-1, keepdims=True)
        yield
        o = jnp.dot(p.astype(BF16), v_ref[:, hsl], preferred_element_type=F32) / denom
        out[:, col0 + h * X_HEAD_DIM:col0 + (h + 1) * X_HEAD_DIM] = (
            o * _silu(z_ref[:, hsl].astype(F32))).astype(out.dtype)
        yield


def _round_robin(*gens):
    gens = list(gens)
    while gens:
        for g in list(gens):
            try:
                next(g)
            except StopIteration:
                gens.remove(g)


def _heads_kernel(sink_ref, relb_ref, bucket_ref, cur_ref, kvp_ref, g_ref, cw_ref, cb_ref, gb_ref, ng_ref, shift_ref,
                  mem_ref, o_ref, tbl, xprev, c_state, n_state, m_state):
    n = pl.program_id(0)

    @pl.when(n == 0)
    def _():
        _swa_build_bias(bucket_ref, relb_ref, tbl)
        for b in range(2):
            _mlstm_reset(xprev.at[b], c_state.at[b], n_state.at[b], m_state.at[b])

    tbl_n = tbl.at[jnp.minimum(n, 1)]

    def pieces(b):
        seg = lambda name: cur_ref.at[b, :, _proj_cols(name)]
        az = _proj_cols("a_z")
        z0 = cur_ref.at[b, :, az.start:az.start + SWA_W // 2]
        z1 = cur_ref.at[b, :, az.start + SWA_W // 2:az.stop]
        rows = slice(b * MEM_LEN, (b + 1) * MEM_LEN)
        out = o_ref.at[b]
        swa = _swa_block(sink_ref, seg("a_q"), z0, z1, kvp_ref.at[b, :, 0:SWA_KV_W], seg("a_k"),
                         kvp_ref.at[b, :, SWA_KV_W:2 * SWA_KV_W], seg("a_v"), tbl_n, out, range(SWA_KV_HEADS))
        mlstm = _mlstm_chunk(seg("m_q"), seg("m_k"), seg("m_v"), g_ref.at[b], seg("m_o"), seg("m_z"),
                             cw_ref, cb_ref, gb_ref, ng_ref, shift_ref,
                             xprev.at[b], c_state.at[b], n_state.at[b], m_state.at[b], out, SWA_W)
        xattn = _xattn_block(seg("c_q"), seg("c_z"), mem_ref.at[rows, 0:X_W], mem_ref.at[rows, X_W:2 * X_W],
                             out, SWA_W + M_V_W)
        return swa, mlstm, xattn

    swa0, mlstm0, xattn0 = pieces(0)
    swa1, mlstm1, xattn1 = pieces(1)
    next(mlstm0)
    next(mlstm1)
    attn = itertools.chain(swa0, xattn0, swa1, xattn1)
    _round_robin(mlstm0, attn, mlstm1, attn)


def _heads(proj, gates, mkv, rel_bias, sinks, conv_w, conv_b, gate_bias, norm_g, *, batch, seq):
    nb = seq // BLOCK
    assert batch == 2 and M_CHUNK == BLOCK and mkv.shape == (batch * MEM_LEN, 2 * X_W)
    kv = _proj_cols("a_k")
    assert _proj_cols("a_v").start == kv.stop and kv.start % (2 * SWA_KV_W) == 0
    kvb = kv.start // (2 * SWA_KV_W)
    cur = lambda n, *_: (0, n, 0)
    prev = lambda n, *_: (0, jnp.maximum(n - 1, 0), kvb)
    const = lambda n, *_: (0, 0)
    proj3 = proj.reshape(batch, seq, PROJ_W)
    grid_spec = pltpu.PrefetchScalarGridSpec(
        num_scalar_prefetch=2,
        grid=(nb,),
        in_specs=[
            pl.BlockSpec((1, 2 * BLOCK), const),
            pl.BlockSpec((batch, BLOCK, PROJ_W), cur),
            pl.BlockSpec((batch, BLOCK, 2 * SWA_KV_W), prev),
            pl.BlockSpec((batch, BLOCK, GATE_W), cur),
            pl.BlockSpec((CONV_WIDTH, 2 * M_QK_W), const),
            pl.BlockSpec((1, 2 * M_QK_W), const),
            pl.BlockSpec((1, GATE_W), const),
            pl.BlockSpec((1, M_V_W), const),
            pl.BlockSpec(((CONV_WIDTH - 1) * M_CHUNK, 2 * M_CHUNK), const),
            pl.BlockSpec((batch * MEM_LEN, 2 * X_W), const),
        ],
        out_specs=pl.BlockSpec((batch, BLOCK, MIX_W), cur),
        scratch_shapes=[
            pltpu.VMEM((2, SWA_HEADS, BLOCK, 2 * BLOCK), F32),
            pltpu.VMEM((batch, M_CHUNK, 2 * M_QK_W), BF16),
            pltpu.VMEM((batch, M_HEADS, M_QK_DIM, M_V_DIM), F32),
            pltpu.VMEM((batch, 8, M_QK_DIM), F32),
            pltpu.VMEM((batch, 8, M_CHUNK), F32),
        ],
    )
    mix = pl.pallas_call(
        _heads_kernel,
        out_shape=jax.ShapeDtypeStruct((batch, seq, MIX_W), BF16),
        grid_spec=grid_spec,
        compiler_params=pltpu.CompilerParams(
            dimension_semantics=("arbitrary",), vmem_limit_bytes=V7X_VMEM_LIMIT),
        name="heads",
    )(sinks, rel_bias.reshape(-1), jnp.asarray(_swa_bucket_row()),
      proj3, proj3, gates.reshape(batch, seq, GATE_W), conv_w, conv_b, gate_bias, norm_g,
      jnp.asarray(_conv_shift_matrix(), BF16), mkv)
    return mix.reshape(batch * seq, MIX_W)


_LANES = 128


def _out_ln_kernel(mix_ref, w_ref, x_ref, xc_ref, lng_ref, lnb_ref, o_ref, rbuf, s1, s2, mean_s, rstd_s, *, n_tiles):
    i, j = pl.program_id(0), pl.program_id(1)
    nj = pl.num_programs(1)
    tm, tn = x_ref.shape
    d_model = rbuf.shape[1]
    col = pl.multiple_of(j * tn, tn)

    @pl.when((i == 0) & (j == 0))
    def _():
        rbuf[...] = jnp.zeros_like(rbuf)
        s1[...] = jnp.zeros_like(s1)
        s2[...] = jnp.zeros_like(s2)
        mean_s[...] = jnp.zeros_like(mean_s)
        rstd_s[...] = jnp.zeros_like(rstd_s)

    def shift():
        return jnp.broadcast_to(DEEPNORM_ALPHA * xc_ref[:, 0:1], (tm, _LANES))

    def stage_c():
        mean, rstd = mean_s[...], rstd_s[...]
        for c in range(tn // _LANES):
            sl = slice(c * _LANES, (c + 1) * _LANES)
            dsl = pl.ds(col + c * _LANES, _LANES)
            o_ref[:, sl] = (rbuf[:, dsl] - mean) * rstd * lng_ref[:, dsl] + lnb_ref[:, dsl]

    def stage_b(n_row_chunks=1):
        rh = tm // n_row_chunks
        for rc in range(n_row_chunks):
            rsl = slice(rc * rh, (rc + 1) * rh)
            r = DEEPNORM_ALPHA * x_ref[rsl, :] + jnp.dot(mix_ref[rsl, :], w_ref[...], preferred_element_type=F32)
            rbuf[rsl, pl.ds(col, tn)] = r
            c0 = DEEPNORM_ALPHA * jnp.broadcast_to(xc_ref[rsl, 0:1], (rh, _LANES))
            a1, a2 = s1[rsl, :], s2[rsl, :]
            for l in range(tn // _LANES):
                d = r[:, l * _LANES:(l + 1) * _LANES] - c0
                a1 = a1 + d
                a2 = a2 + d * d
            s1[rsl, :] = a1
            s2[rsl, :] = a2

    @pl.when(i < n_tiles)
    def _():
        stage_c()
        stage_b()

    @pl.when(i == n_tiles)
    def _():
        stage_c()

    @pl.when(j == nj - 1)
    def _():
        mean_d = jnp.sum(s1[...], axis=-1, keepdims=True) * (1.0 / d_model)
        var = jnp.sum(s2[...], axis=-1, keepdims=True) * (1.0 / d_model) - mean_d * mean_d
        mean_s[...] = shift() + jnp.broadcast_to(mean_d, (tm, _LANES))
        rstd_s[...] = jnp.broadcast_to(lax.rsqrt(var + LN_EPS), (tm, _LANES))
        s1[...] = jnp.zeros_like(s1)
        s2[...] = jnp.zeros_like(s2)


def _out_ln(mix, w_out_bf, x2d, ln_g, ln_b, *, tm=1024, tn=512):
    T, D = x2d.shape
    n_tiles, nj = T // tm, D // tn
    assert T % tm == 0 and D % tn == 0 and mix.shape == (T, D)
    row = lambda i: jnp.minimum(i, n_tiles - 1)
    const = lambda i, j: (0, 0)
    return pl.pallas_call(
        functools.partial(_out_ln_kernel, n_tiles=n_tiles),
        out_shape=jax.ShapeDtypeStruct((T, D), F32),
        grid=(n_tiles + 1, nj),
        in_specs=[
            pl.BlockSpec((tm, D), lambda i, j: (row(i), 0)),
            pl.BlockSpec((D, tn), lambda i, j: (0, j)),
            pl.BlockSpec((tm, tn), lambda i, j: (row(i), j)),
            pl.BlockSpec((tm, _LANES), lambda i, j: (row(i), 0)),
            pl.BlockSpec((1, D), const),
            pl.BlockSpec((1, D), const),
        ],
        out_specs=pl.BlockSpec((tm, tn), lambda i, j: (jnp.maximum(i - 1, 0), jnp.where(i < 1, 0, j))),
        scratch_shapes=[
            pltpu.VMEM((tm, D), F32),
            pltpu.VMEM((tm, _LANES), F32),
            pltpu.VMEM((tm, _LANES), F32),
            pltpu.VMEM((tm, _LANES), F32),
            pltpu.VMEM((tm, _LANES), F32),
        ],
        compiler_params=pltpu.CompilerParams(
            dimension_semantics=("arbitrary", "arbitrary"), vmem_limit_bytes=V7X_VMEM_LIMIT),
        name="out_ln",
    )(mix, w_out_bf, x2d, x2d, ln_g, ln_b)


def kernel(x, mem, w_in, conv_w, conv_b, b_i, b_f, m_norm_g, rel_bias, sinks, w_mem_kv, w_out, ln_g, ln_b):
    B, S, D = x.shape
    assert D == D_MODEL and S % BLOCK == 0 and S % M_CHUNK == 0
    T = B * S
    x2d = x.reshape(T, D)
    gate_bias = jnp.concatenate(
        [b_i.astype(F32), b_f.astype(F32), jnp.zeros((GATE_W - GATE_COLS,), F32)])[None, :]

    proj, gates, w_out_bf = _in_proj(x2d.astype(BF16), w_in.T, w_out)
    mkv = _mem_kv(mem.reshape(B * MEM_LEN, D), w_mem_kv)
    mix = _heads(proj, gates, mkv, rel_bias.astype(F32), sinks.astype(F32), conv_w.astype(F32),
                 conv_b.astype(F32)[None, :], gate_bias, m_norm_g.astype(F32)[None, :], batch=B, seq=S)
    out = _out_ln(mix, w_out_bf, x2d, ln_g.astype(F32)[None, :], ln_b.astype(F32)[None, :])
    return out.reshape(B, S, D).astype(x.dtype)
```
